```python
import math
import jax, jax.numpy as jnp
from jax import lax
import numpy as np

D_MODEL = 2048
BATCH = 4
SEQ = 4096
DEPTH = 4

CTX_LEN = 256
GRID_W = 64
N_MIXERS = 3
CHUNK = 64
CONV_W = 4
EPS = 1e-6
D_FF = 4 * D_MODEL

GDN_QK_HEADS = D_MODEL // 128
GDN_V_HEADS = 2 * GDN_QK_HEADS
GDN_DK = 128
GDN_DV = 128
GDN_QK = GDN_QK_HEADS * GDN_DK
GDN_VW = GDN_V_HEADS * GDN_DV
GDN_REP = GDN_V_HEADS // GDN_QK_HEADS
GDN_IN = 2 * GDN_QK + 2 * GDN_VW + 4 * GDN_V_HEADS

ML_HEADS = 8
ML_DQK = D_MODEL // (2 * ML_HEADS)
ML_DV = D_MODEL // ML_HEADS
ML_QK = ML_HEADS * ML_DQK
ML_V = ML_HEADS * ML_DV
ML_IN = 2 * ML_QK + 2 * ML_V + 4 * ML_HEADS
GATE_CAP = 15.0

LRU_W = D_MODEL
LRU_BLOCKS = 8
LRU_BW = LRU_W // LRU_BLOCKS
LRU_C = 8.0

kernel_name = "hybrid_gdn_mlstm_rglru_prefix_dit"


def rmsnorm(x, w):
    xf = x.astype(jnp.float32)
    y = xf * lax.rsqrt(jnp.mean(xf * xf, axis=-1, keepdims=True) + EPS)
    return (y * w.astype(jnp.float32)).astype(x.dtype)


def modulate(h, shift, scale):
    return h * (1 + scale[:, None, :]) + shift[:, None, :]


def l2norm(t):
    return t * lax.rsqrt(jnp.sum(t * t, axis=-1, keepdims=True) + EPS)


def soft_cap(t):
    return GATE_CAP * jnp.tanh(t / GATE_CAP)


def dwconv(x, w):
    K, C = w.shape
    left = K // 2
    return lax.conv_general_dilated(x, w[:, None, :].astype(x.dtype), window_strides=(1,),
                                    padding=[(left, K - 1 - left)],
                                    dimension_numbers=('NWC', 'WIO', 'NWC'), feature_group_count=C)


def split_heads(t, n):
    B, S, _ = t.shape
    return t.reshape(B, S, n, -1).transpose(0, 2, 1, 3)


def to_chunks(t):
    B, H, S = t.shape[:3]
    t = t.reshape(B, H, S // CHUNK, CHUNK, *t.shape[3:])
    return jnp.moveaxis(t, 2, 0)


def from_chunks(t):
    t = jnp.moveaxis(t, 0, 2)
    return t.reshape(t.shape[0], t.shape[1], -1, *t.shape[4:])


def to_col_major(x):
    B, S, D = x.shape
    rows = S // GRID_W
    return x.reshape(B, rows, GRID_W, D).transpose(0, 2, 1, 3).reshape(B, S, D)


def from_col_major(x):
    B, S, D = x.shape
    rows = S // GRID_W
    return x.reshape(B, GRID_W, rows, D).transpose(0, 2, 1, 3).reshape(B, S, D)


def bidirectional(scan_fn, ctx_f, lat_f, ctx_b, lat_b, init, axis):
    rev = lambda ts: tuple(jnp.flip(t, axis) for t in ts)
    yc_f, st_f = scan_fn(*ctx_f, init)
    yl_f, _ = scan_fn(*lat_f, st_f)
    yc_b, st_b = scan_fn(*rev(ctx_b), init)
    yl_b, _ = scan_fn(*rev(lat_b), st_b)
    return yc_f + jnp.flip(yc_b, axis), yl_f + jnp.flip(yl_b, axis)


def gdn_chunked(q, k, v, g, beta, state0):
    q, k, v, g, beta = map(to_chunks, (q, k, v, g, beta))
    L = CHUNK
    incl = jnp.tril(jnp.ones((L, L), bool))
    strict = jnp.tril(jnp.ones((L, L), bool), -1)
    g = jnp.cumsum(g, axis=-1)
    decay = jnp.exp(jnp.where(incl, g[..., :, None] - g[..., None, :], -jnp.inf))
    kb = k * beta[..., None]
    A = jnp.where(strict, jnp.einsum('nbhid,nbhjd->nbhij', kb, k) * decay, 0.0) + jnp.eye(L, dtype=jnp.float32)
    u = lax.linalg.triangular_solve(A, v * beta[..., None], left_side=True, lower=True, unit_diagonal=True)
    w = lax.linalg.triangular_solve(A, kb * jnp.exp(g)[..., None], left_side=True, lower=True, unit_diagonal=True)
    a_qk = jnp.einsum('nbhid,nbhjd->nbhij', q, k) * decay

    def step(S, xs):
        qc, kc, uc, wc, gc, ac = xs
        v_new = uc - wc @ S
        o = (qc * jnp.exp(gc)[..., None]) @ S + ac @ v_new
        gl = gc[..., -1:]
        S = S * jnp.exp(gl)[..., None] + jnp.einsum('bhld,bhle->bhde', kc * jnp.exp(gl - gc)[..., None], v_new)
        return S, o

    S_fin, o = lax.scan(step, state0, (q, k, u, w, g, a_qk))
    return from_chunks(o), S_fin


def gdn_mixer(h_ctx, h_lat, w_in, conv_w, a_log, dt_bias, norm_w, w_out):
    f32 = jnp.float32

    def prep(h):
        B, S, _ = h.shape
        p = h @ w_in
        qkv = jax.nn.silu(dwconv(p[..., :2 * GDN_QK + GDN_VW], conv_w)).astype(f32)
        z = p[..., 2 * GDN_QK + GDN_VW:2 * GDN_QK + 2 * GDN_VW].reshape(B, S, GDN_V_HEADS, GDN_DV)
        q = jnp.repeat(l2norm(split_heads(qkv[..., :GDN_QK], GDN_QK_HEADS)), GDN_REP, axis=1) * GDN_DK ** -0.5
        k = jnp.repeat(l2norm(split_heads(qkv[..., GDN_QK:2 * GDN_QK], GDN_QK_HEADS)), GDN_REP, axis=1)
        v = split_heads(qkv[..., 2 * GDN_QK:], GDN_V_HEADS)
        ba = p[..., 2 * GDN_QK + 2 * GDN_VW:].astype(f32).reshape(B, S, 2, 2, GDN_V_HEADS)
        beta = jax.nn.sigmoid(ba[:, :, 0]).transpose(0, 2, 3, 1)
        g = -jnp.exp(a_log.astype(f32))[:, :, None] * jax.nn.softplus(
            ba[:, :, 1].transpose(0, 2, 3, 1) + dt_bias.astype(f32)[:, :, None])
        return q, k, v, beta, g, z

    qc, kc, vc, bc, gc, zc = prep(h_ctx)
    ql, kl, vl, bl, gl, zl = prep(h_lat)
    init = jnp.zeros((h_ctx.shape[0], GDN_V_HEADS, GDN_DK, GDN_DV), f32)
    oc, ol = bidirectional(gdn_chunked,
                           (qc, kc, vc, gc[:, 0], bc[:, 0]), (ql, kl, vl, gl[:, 0], bl[:, 0]),
                           (qc, kc, vc, gc[:, 1], bc[:, 1]), (ql, kl, vl, gl[:, 1], bl[:, 1]),
                           init, axis=2)

    def out(o, z, h):
        B, S, _ = h.shape
        y = rmsnorm(o.transpose(0, 2, 1, 3), norm_w) * jax.nn.silu(z.astype(f32))
        return y.reshape(B, S, GDN_VW).astype(h.dtype) @ w_out

    return out(oc, zc, h_ctx), out(ol, zl, h_lat)


def mlstm_chunked(q, k, v, ig, lf, state):
    L = CHUNK
    causal = jnp.tril(jnp.ones((L, L), bool))

    def step(carry, xs):
        C, n, m = carry
        qc, kc, vc, ic, fc = xs
        b = jnp.cumsum(fc, axis=-1)
        d_log = jnp.where(causal, b[..., :, None] - b[..., None, :] + ic[..., None, :], -jnp.inf)
        inter = b + m[..., None]
        m_t = jnp.maximum(inter, jnp.max(d_log, axis=-1))
        s = jnp.einsum('bhld,bhsd->bhls', qc, kc) * jnp.exp(d_log - m_t[..., None])
        w_inter = jnp.exp(inter - m_t)
        num = w_inter[..., None] * jnp.einsum('bhld,bhde->bhle', qc, C) + jnp.einsum('bhls,bhse->bhle', s, vc)
        den = w_inter * jnp.einsum('bhld,bhd->bhl', qc, n) + jnp.sum(s, axis=-1)
        h = num / jnp.maximum(jnp.abs(den), jnp.exp(-m_t))[..., None]
        b_last = b[..., -1]
        w_log = b_last[..., None] - b + ic
        m_new = jnp.maximum(b_last + m, jnp.max(w_log, axis=-1))
        w_k = jnp.exp(w_log - m_new[..., None])
        carry_decay = jnp.exp(b_last + m - m_new)
        C = carry_decay[..., None, None] * C + jnp.einsum('bhl,bhld,bhle->bhde', w_k, kc, vc)
        n = carry_decay[..., None] * n + jnp.einsum('bhl,bhld->bhd', w_k, kc)
        return (C, n, m_new), h

    state, h = lax.scan(step, state, tuple(map(to_chunks, (q, k, v, ig, lf))))
    return from_chunks(h), state


def mlstm_mixer(h_ctx, h_lat, w_in, gate_b, norm_w, w_out):
    f32 = jnp.float32

    def prep(h):
        B, S, _ = h.shape
        p = h @ w_in
        q = split_heads(p[..., :ML_QK], ML_HEADS).astype(f32) * ML_DQK ** -0.5
        k = split_heads(p[..., ML_QK:2 * ML_QK], ML_HEADS).astype(f32)
        v = split_heads(p[..., 2 * ML_QK:2 * ML_QK + ML_V], ML_HEADS).astype(f32)
        o = p[..., 2 * ML_QK + ML_V:2 * ML_QK + 2 * ML_V].reshape(B, S, ML_HEADS, ML_DV)
        gt = soft_cap(p[..., 2 * ML_QK + 2 * ML_V:].astype(f32).reshape(B, S, 2, 2, ML_HEADS)
                      + gate_b.astype(f32)).transpose(0, 2, 3, 4, 1)
        ig = gt[:, :, 0]
        lf = jax.nn.log_sigmoid(gt[:, :, 1])
        return q, k, v, ig, lf, o

    qc, kc, vc, ic, fc, oc = prep(h_ctx)
    ql, kl, vl, il, fl, ol = prep(h_lat)
    B = h_ctx.shape[0]
    init = (jnp.zeros((B, ML_HEADS, ML_DQK, ML_DV), f32), jnp.zeros((B, ML_HEADS, ML_DQK), f32),
            jnp.zeros((B, ML_HEADS), f32))
    hc, hl = bidirectional(mlstm_chunked,
                           (qc, kc, vc, ic[:, 0], fc[:, 0]), (ql, kl, vl, il[:, 0], fl[:, 0]),
                           (qc, kc, vc, ic[:, 1], fc[:, 1]), (ql, kl, vl, il[:, 1], fl[:, 1]),
                           init, axis=2)

    def out(hh, o, h):
        B_, S, _ = h.shape
        y = rmsnorm(hh.transpose(0, 2, 1, 3), norm_w.reshape(ML_HEADS, ML_DV)) * jax.nn.sigmoid(o.astype(f32))
        return y.reshape(B_, S, ML_V).astype(h.dtype) @ w_out

    return out(hc, oc, h_ctx), out(hl, ol, h_lat)


def lru_scan(a, u, h0):
    def combine(left, right):
        return left[0] * right[0], right[0] * left[1] + right[1]
    a_cum, h = lax.associative_scan(combine, (a, u), axis=1)
    h = h + a_cum * h0[:, None, :]
    return h, h[:, -1]


def lru_mixer(h_ctx, h_lat, w_in, conv_w, conv_b, w_gate, b_gate, lam, w_out):
    f32 = jnp.float32

    def prep(h):
        B, S, _ = h.shape
        p = h @ w_in
        y = jax.nn.gelu(p[..., :LRU_W])
        xr = (dwconv(p[..., LRU_W:], conv_w) + conv_b).astype(f32)
        gt = jnp.einsum('bsnk,dgnkj->dgbsnj', xr.reshape(B, S, LRU_BLOCKS, LRU_BW), w_gate.astype(f32))
        gt = gt.reshape(2, 2, B, S, LRU_W) + b_gate.astype(f32)[:, :, None, None, :]
        log_a = -LRU_C * jax.nn.sigmoid(gt[:, 0]) * jax.nn.softplus(-lam.astype(f32))[:, None, None, :]
        a = jnp.exp(log_a)
        u = jnp.sqrt(-jnp.expm1(2.0 * log_a)) * jax.nn.sigmoid(gt[:, 1]) * xr[None]
        return y, a, u

    yc, ac, uc = prep(h_ctx)
    yl, al, ul = prep(h_lat)
    init = jnp.zeros((h_ctx.shape[0], LRU_W), f32)
    hc, hl = bidirectional(lru_scan, (ac[0], uc[0]), (al[0], ul[0]), (ac[1], uc[1]), (al[1], ul[1]), init, axis=1)
    return (hc * yc).astype(h_ctx.dtype) @ w_out, (hl * yl).astype(h_lat.dtype) @ w_out


def sqrelu_mlp(h, w_up, w_down):
    return jnp.square(jax.nn.relu(h @ w_up)) @ w_down


def setup_inputs(seed: int = 0) -> dict:
    key = jax.random.key(seed)
    ks = iter(jax.random.split(key, 40))
    f32 = jnp.float32
    nrm = lambda shape, scale: jax.random.normal(next(ks), shape, f32) * scale
    uni = lambda shape, lo, hi: jax.random.uniform(next(ks), shape, f32, lo, hi)
    D = D_MODEL
    nA = len(range(0, DEPTH, N_MIXERS))
    nB = len(range(1, DEPTH, N_MIXERS))
    nC = len(range(2, DEPTH, N_MIXERS))
    inp = {}
    inp["x"] = nrm((BATCH, SEQ, D), 1.0)
    inp["c"] = nrm((BATCH, D), 1.0)
    inp["ctx"] = nrm((BATCH, CTX_LEN, D), 1.0)
    inp["c_ctx"] = nrm((D,), 1.0)
    inp["w_mod"] = nrm((DEPTH, D, 6 * D), 0.5 * D ** -0.5)
    inp["b_mod"] = nrm((DEPTH, 6 * D), 0.02)
    inp["norm_mix"] = 1.0 + nrm((DEPTH, D), 0.05)
    inp["norm_ff"] = 1.0 + nrm((DEPTH, D), 0.05)
    inp["norm_out"] = 1.0 + nrm((D,), 0.05)
    inp["gdn_w_in"] = nrm((nA, D, GDN_IN), D ** -0.5)
    inp["gdn_conv"] = nrm((nA, CONV_W, 2 * GDN_QK + GDN_VW), CONV_W ** -0.5)
    inp["gdn_a_log"] = jnp.log(uni((nA, 2, GDN_V_HEADS), 1.0, 16.0))
    dt = jnp.exp(uni((nA, 2, GDN_V_HEADS), math.log(1e-3), math.log(1e-1)))
    inp["gdn_dt_bias"] = dt + jnp.log(-jnp.expm1(-dt))
    inp["gdn_norm"] = 1.0 + nrm((nA, GDN_DV), 0.05)
    inp["gdn_w_out"] = nrm((nA, GDN_VW, D), GDN_VW ** -0.5)
    inp["ml_w_in"] = nrm((nB, D, ML_IN), D ** -0.5)
    i_bias = nrm((nB, 2, ML_HEADS), 0.5)
    f_bias = uni((nB, 2, ML_HEADS), 3.0, 6.0)
    inp["ml_gate_b"] = jnp.stack([i_bias, f_bias], axis=2)
    inp["ml_norm"] = 1.0 + nrm((nB, ML_V), 0.05)
    inp["ml_w_out"] = nrm((nB, ML_V, D), ML_V ** -0.5)
    inp["lru_w_in"] = nrm((nC, D, 2 * LRU_W), D ** -0.5)
    inp["lru_conv"] = nrm((nC, CONV_W, LRU_W), CONV_W ** -0.5)
    inp["lru_conv_b"] = nrm((nC, LRU_W), 0.02)
    inp["lru_w_gate"] = nrm((nC, 2, 2, LRU_BLOCKS, LRU_BW, LRU_BW), LRU_BW ** -0.5)
    inp["lru_b_gate"] = nrm((nC, 2, 2, LRU_W), 0.02)
    s = uni((nC, 2, LRU_W), 0.9, 0.999) ** (1.0 / LRU_C)
    inp["lru_lambda"] = jnp.log(s) - jnp.log1p(-s)
    inp["lru_w_out"] = nrm((nC, LRU_W, D), LRU_W ** -0.5)
    inp["ff_w_up"] = nrm((DEPTH, D, D_FF), D ** -0.5)
    inp["ff_w_down"] = nrm((DEPTH, D_FF, D), D_FF ** -0.5)
    return inp


def reference(x, c, ctx, c_ctx, w_mod, b_mod, norm_mix, norm_ff, norm_out,
              gdn_w_in, gdn_conv, gdn_a_log, gdn_dt_bias, gdn_norm, gdn_w_out,
              ml_w_in, ml_gate_b, ml_norm, ml_w_out,
              lru_w_in, lru_conv, lru_conv_b, lru_w_gate, lru_b_gate, lru_lambda, lru_w_out,
              ff_w_up, ff_w_down):
    sc_lat = jax.nn.silu(c)
    sc_ctx = jax.nn.silu(c_ctx)[None]
    lat, cx = x, ctx
    for i in range(DEPTH):
        kind, j = i % N_MIXERS, i // N_MIXERS
        sh1_l, sc1_l, g1_l, sh2_l, sc2_l, g2_l = jnp.split(sc_lat @ w_mod[i] + b_mod[i], 6, axis=-1)
        sh1_c, sc1_c, g1_c, sh2_c, sc2_c, g2_c = jnp.split(sc_ctx @ w_mod[i] + b_mod[i], 6, axis=-1)
        hc = modulate(rmsnorm(cx, norm_mix[i]), sh1_c, sc1_c)
        hl = modulate(rmsnorm(lat, norm_mix[i]), sh1_l, sc1_l)
        col = (i % 2 == 1)
        if col:
            hl = to_col_major(hl)
        if kind == 0:
            yc, yl = gdn_mixer(hc, hl, gdn_w_in[j], gdn_conv[j], gdn_a_log[j], gdn_dt_bias[j],
                               gdn_norm[j], gdn_w_out[j])
        elif kind == 1:
            yc, yl = mlstm_mixer(hc, hl, ml_w_in[j], ml_gate_b[j], ml_norm[j], ml_w_out[j])
        else:
            yc, yl = lru_mixer(hc, hl, lru_w_in[j], lru_conv[j], lru_conv_b[j], lru_w_gate[j],
                               lru_b_gate[j], lru_lambda[j], lru_w_out[j])
        if col:
            yl = from_col_major(yl)
        lat = lat + g1_l[:, None, :] * yl
        lat = lat + g2_l[:, None, :] * sqrelu_mlp(modulate(rmsnorm(lat, norm_ff[i]), sh2_l, sc2_l),
                                                  ff_w_up[i], ff_w_down[i])
        if i < DEPTH - 1:
            cx = cx + g1_c[:, None, :] * yc
            cx = cx + g2_c[:, None, :] * sqrelu_mlp(modulate(rmsnorm(cx, norm_ff[i]), sh2_c, sc2_c),
                                                    ff_w_up[i], ff_w_down[i])
    return rmsnorm(lat, norm_out)
```

```python
import functools
import math

import jax
import jax.numpy as jnp
from jax import lax
from jax.experimental import pallas as pl
from jax.experimental.pallas import tpu as pltpu

F32 = jnp.float32
BF16 = jnp.bfloat16

D_MODEL = 2048
BATCH = 4
SEQ = 4096
DEPTH = 4
CTX_LEN = 256
GRID_W = 64
N_MIXERS = 3
CHUNK = 64
CONV_W = 4
EPS = 1e-6
D_FF = 4 * D_MODEL

GDN_QK_HEADS = D_MODEL // 128
GDN_V_HEADS = 2 * GDN_QK_HEADS
GDN_DK = 128
GDN_DV = 128
GDN_QK = GDN_QK_HEADS * GDN_DK
GDN_VW = GDN_V_HEADS * GDN_DV
GDN_REP = GDN_V_HEADS // GDN_QK_HEADS
GDN_IN = 2 * GDN_QK + 2 * GDN_VW + 4 * GDN_V_HEADS

ML_HEADS = 8
ML_DQK = D_MODEL // (2 * ML_HEADS)
ML_DV = D_MODEL // ML_HEADS
ML_QK = ML_HEADS * ML_DQK
ML_V = ML_HEADS * ML_DV
ML_IN = 2 * ML_QK + 2 * ML_V + 4 * ML_HEADS
GATE_CAP = 15.0

LRU_W = D_MODEL
LRU_BLOCKS = 8
LRU_BW = LRU_W // LRU_BLOCKS
LRU_C = 8.0

N_LAT = BATCH * SEQ
N_CTX = BATCH * CTX_LEN
T_ROWS = N_LAT + N_CTX
MOD_ROWS = 8
CTX_MOD_ROW = BATCH

V7X_VMEM_BYTES = 64 * 1024 * 1024
VMEM_LIMIT = 56 * 1024 * 1024
SUBLANES = 8
LANES = 128

ROW_TILE = 1024
MLP_ROW_TILE = 512
COL_TILE = 512
SEQ_TILE = 256


def _params(*sem):
    return pltpu.CompilerParams(dimension_semantics=sem, vmem_limit_bytes=VMEM_LIMIT)


def _mod_row(tile, tile_rows):
    lat_tiles = N_LAT // tile_rows
    return jnp.where(tile < lat_tiles, tile // (SEQ // tile_rows), CTX_MOD_ROW)


def _mod_kernel(c_ref, w_ref, b_ref, o_ref):
    c = c_ref[...]
    s = c * jax.nn.sigmoid(c)
    o_ref[0] = jnp.dot(s.astype(BF16), w_ref[0].astype(BF16), preferred_element_type=F32) + b_ref[0]


def modulation_table(c_all, w_mod, b_mod):
    tn = 1024
    n6 = 6 * D_MODEL
    return pl.pallas_call(
        _mod_kernel,
        grid=(DEPTH, n6 // tn),
        in_specs=[pl.BlockSpec((MOD_ROWS, D_MODEL), lambda i, n: (0, 0)),
                  pl.BlockSpec((1, D_MODEL, tn), lambda i, n: (i, 0, n)),
                  pl.BlockSpec((1, 1, tn), lambda i, n: (i, 0, n))],
        out_specs=pl.BlockSpec((1, MOD_ROWS, tn), lambda i, n: (i, 0, n)),
        out_shape=jax.ShapeDtypeStruct((DEPTH, MOD_ROWS, n6), F32),
        compiler_params=_params("arbitrary", "arbitrary"),
        name="modulation_table",
    )(c_all, w_mod, b_mod.reshape(DEPTH, 1, n6))


def _norm_modulate(x, nw, shift, scale):
    y = x * lax.rsqrt(jnp.mean(x * x, axis=-1, keepdims=True) + EPS)
    return (y * nw) * (1.0 + scale) + shift


def _in_proj_kernel(x_ref, nw_ref, sh_ref, sc_ref, w_ref, o_ref, h_ref, *, tm):
    row = _mod_row(pl.program_id(0), tm)

    @pl.when(pl.program_id(1) == 0)
    def _():
        h = _norm_modulate(x_ref[...], nw_ref[...], sh_ref[pl.ds(row, 1), :], sc_ref[pl.ds(row, 1), :])
        h_ref[...] = h.astype(BF16)

    o_ref[...] = jnp.dot(h_ref[...], w_ref[...], preferred_element_type=F32)


def in_proj(r, norm_w, mod_i, w_bf16):
    n = w_bf16.shape[1]
    tm, tn = ROW_TILE, COL_TILE
    return pl.pallas_call(
        functools.partial(_in_proj_kernel, tm=tm),
        grid=(T_ROWS // tm, n // tn),
        in_specs=[pl.BlockSpec((tm, D_MODEL), lambda t, j: (t, 0)),
                  pl.BlockSpec((1, D_MODEL), lambda t, j: (0, 0)),
                  pl.BlockSpec((MOD_ROWS, D_MODEL), lambda t, j: (0, 0)),
                  pl.BlockSpec((MOD_ROWS, D_MODEL), lambda t, j: (0, 1)),
                  pl.BlockSpec((D_MODEL, tn), lambda t, j: (0, j))],
        out_specs=pl.BlockSpec((tm, tn), lambda t, j: (t, j)),
        out_shape=jax.ShapeDtypeStruct((T_ROWS, n), F32),
        scratch_shapes=[pltpu.VMEM((tm, D_MODEL), BF16)],
        compiler_params=_params("arbitrary", "arbitrary"),
        name="in_proj",
    )(r, norm_w.reshape(1, D_MODEL), mod_i, mod_i, w_bf16)


def _out_proj_kernel(y_ref, w_ref, r_ref, g_ref, o_ref, *, tm):
    row = _mod_row(pl.program_id(0), tm)
    acc = jnp.dot(y_ref[...], w_ref[...], preferred_element_type=F32)
    o_ref[...] = r_ref[...] + g_ref[pl.ds(row, 1), :] * acc


def out_proj(y_bf16, w_bf16, r, mod_i):
    k = y_bf16.shape[1]
    tm, tn = ROW_TILE, COL_TILE
    nj = D_MODEL // tn
    return pl.pallas_call(
        functools.partial(_out_proj_kernel, tm=tm),
        grid=(T_ROWS // tm, nj),
        in_specs=[pl.BlockSpec((tm, k), lambda t, j: (t, 0)),
                  pl.BlockSpec((k, tn), lambda t, j: (0, j)),
                  pl.BlockSpec((tm, tn), lambda t, j: (t, j)),
                  pl.BlockSpec((MOD_ROWS, tn), lambda t, j: (0, 2 * nj + j))],
        out_specs=pl.BlockSpec((tm, tn), lambda t, j: (t, j)),
        out_shape=jax.ShapeDtypeStruct((T_ROWS, D_MODEL), F32),
        compiler_params=_params("arbitrary", "arbitrary"),
        name="out_proj",
    )(y_bf16, w_bf16, r, mod_i)


def _mlp_kernel(x_ref, nw_ref, sh_ref, sc_ref, g_ref, wu_ref, wd_ref, o_ref, h_ref, acc_ref, *, tm):
    f = pl.program_id(1)
    row = _mod_row(pl.program_id(0), tm)

    @pl.when(f == 0)
    def _():
        h = _norm_modulate(x_ref[...], nw_ref[...], sh_ref[pl.ds(row, 1), :], sc_ref[pl.ds(row, 1), :])
        h_ref[...] = h.astype(BF16)
        acc_ref[...] = jnp.zeros_like(acc_ref)

    u = jnp.dot(h_ref[...], wu_ref[...], preferred_element_type=F32)
    a = jnp.square(jnp.maximum(u, 0.0))
    acc_ref[...] += jnp.dot(a.astype(BF16), wd_ref[...], preferred_element_type=F32)

    @pl.when(f == pl.num_programs(1) - 1)
    def _():
        o_ref[...] = x_ref[...] + g_ref[pl.ds(row, 1), :] * acc_ref[...]


def mlp(r, norm_w, mod_i, wu_bf16, wd_bf16):
    tm, tf = MLP_ROW_TILE, 1024
    return pl.pallas_call(
        functools.partial(_mlp_kernel, tm=tm),
        grid=(T_ROWS // tm, D_FF // tf),
        in_specs=[pl.BlockSpec((tm, D_MODEL), lambda t, f: (t, 0)),
                  pl.BlockSpec((1, D_MODEL), lambda t, f: (0, 0)),
                  pl.BlockSpec((MOD_ROWS, D_MODEL), lambda t, f: (0, 3)),
                  pl.BlockSpec((MOD_ROWS, D_MODEL), lambda t, f: (0, 4)),
                  pl.BlockSpec((MOD_ROWS, D_MODEL), lambda t, f: (0, 5)),
                  pl.BlockSpec((D_MODEL, tf), lambda t, f: (0, f)),
                  pl.BlockSpec((tf, D_MODEL), lambda t, f: (f, 0))],
        out_specs=pl.BlockSpec((tm, D_MODEL), lambda t, f: (t, 0)),
        out_shape=jax.ShapeDtypeStruct((T_ROWS, D_MODEL), F32),
        scratch_shapes=[pltpu.VMEM((tm, D_MODEL), BF16), pltpu.VMEM((tm, D_MODEL), F32)],
        compiler_params=_params("arbitrary", "arbitrary"),
        name="mlp",
    )(r, norm_w.reshape(1, D_MODEL), mod_i, mod_i, mod_i, wu_bf16, wd_bf16)


def _final_norm_kernel(x_ref, w_ref, o_ref):
    x = x_ref[...]
    o_ref[...] = x * lax.rsqrt(jnp.mean(x * x, axis=-1, keepdims=True) + EPS) * w_ref[...]


def final_norm(r, w):
    tm = 512
    return pl.pallas_call(
        _final_norm_kernel,
        grid=(N_LAT // tm,),
        in_specs=[pl.BlockSpec((tm, D_MODEL), lambda t: (t, 0)),
                  pl.BlockSpec((1, D_MODEL), lambda t: (0, 0))],
        out_specs=pl.BlockSpec((tm, D_MODEL), lambda t: (t, 0)),
        out_shape=jax.ShapeDtypeStruct((N_LAT, D_MODEL), F32),
        compiler_params=_params("arbitrary"),
        name="final_norm",
    )(r, w.reshape(1, D_MODEL))


LAT_TILES = SEQ // SEQ_TILE
N_SEQ_TILES = T_ROWS // SEQ_TILE
HALO_BLOCKS = SEQ_TILE // SUBLANES


def _seg_first(t):
    return jnp.logical_or(t >= BATCH * LAT_TILES, t % LAT_TILES == 0)


def _seg_last(t):
    return jnp.logical_or(t >= BATCH * LAT_TILES, t % LAT_TILES == LAT_TILES - 1)


def _conv4(ext_ref, cw_ref, tm):
    acc = cw_ref[0:1, :] * ext_ref[pl.ds(SUBLANES - 2, tm), :]
    acc = acc + cw_ref[1:2, :] * ext_ref[pl.ds(SUBLANES - 1, tm), :]
    acc = acc + cw_ref[2:3, :] * ext_ref[pl.ds(SUBLANES, tm), :]
    acc = acc + cw_ref[3:4, :] * ext_ref[pl.ds(SUBLANES + 1, tm), :]
    return acc


def _fill_halo(ext_ref, prev_ref, cur_ref, next_ref, t, tm):
    zero = jnp.zeros((SUBLANES, ext_ref.shape[1]), F32)
    ext_ref[pl.ds(0, SUBLANES), :] = jnp.where(_seg_first(t), zero, prev_ref[...])
    ext_ref[pl.ds(SUBLANES, tm), :] = cur_ref[...]
    ext_ref[pl.ds(SUBLANES + tm, SUBLANES), :] = jnp.where(_seg_last(t), zero, next_ref[...])


def _halo_specs(width, col):
    last_blk = T_ROWS // SUBLANES - 1
    return [pl.BlockSpec((SUBLANES, width), lambda t, *_: (jnp.maximum(t * HALO_BLOCKS - 1, 0), col)),
            pl.BlockSpec((SEQ_TILE, width), lambda t, *_: (t, col)),
            pl.BlockSpec((SUBLANES, width), lambda t, *_: (jnp.minimum((t + 1) * HALO_BLOCKS, last_blk), col))]


def _lru_prep_kernel(py_ref, xp_ref, xc_ref, xn_ref, cw_ref, cb_ref, wg_ref, bg_ref, lam_ref,
                     y_ref, a_ref, u_ref, ext_ref):
    t = pl.program_id(0)
    tm = SEQ_TILE
    y_ref[...] = jax.nn.gelu(py_ref[...])
    _fill_halo(ext_ref, xp_ref, xc_ref, xn_ref, t, tm)
    xr = _conv4(ext_ref, cw_ref, tm) + cb_ref[...]
    xb = xr.astype(BF16)
    for d in range(2):
        gts = []
        for g in range(2):
            cols = []
            for n in range(LRU_BLOCKS):
                cols.append(jnp.dot(xb[:, n * LRU_BW:(n + 1) * LRU_BW], wg_ref[d, g, n],
                                    preferred_element_type=F32))
            gts.append(jnp.concatenate(cols, axis=1) + bg_ref[d, g])
        log_a = -LRU_C * jax.nn.sigmoid(gts[0]) * jax.nn.softplus(-lam_ref[d])
        a = jnp.exp(log_a)
        a_ref[d] = a
        one_minus_a2 = -jnp.tanh(log_a) * (a * a + 1.0)
        u_ref[d] = jnp.sqrt(one_minus_a2) * jax.nn.sigmoid(gts[1]) * xr


def lru_prep(p, conv_w, conv_b, w_gate_bf16, b_gate, lam):
    w = LRU_W
    full = lambda shape: pl.BlockSpec(shape, lambda t: (0,) * len(shape))
    return pl.pallas_call(
        _lru_prep_kernel,
        grid=(N_SEQ_TILES,),
        in_specs=[pl.BlockSpec((SEQ_TILE, w), lambda t: (t, 0))] + _halo_specs(w, 1) + [
            full((CONV_W, w)), full((1, w)),
            full((2, 2, LRU_BLOCKS, LRU_BW, LRU_BW)), full((2, 2, 1, w)), full((2, 1, w))],
        out_specs=[pl.BlockSpec((SEQ_TILE, w), lambda t: (t, 0)),
                   pl.BlockSpec((2, SEQ_TILE, w), lambda t: (0, t, 0)),
                   pl.BlockSpec((2, SEQ_TILE, w), lambda t: (0, t, 0))],
        out_shape=[jax.ShapeDtypeStruct((T_ROWS, w), F32),
                   jax.ShapeDtypeStruct((2, T_ROWS, w), F32),
                   jax.ShapeDtypeStruct((2, T_ROWS, w), F32)],
        scratch_shapes=[pltpu.VMEM((SEQ_TILE + 2 * SUBLANES, w), F32)],
        compiler_params=_params("arbitrary"),
        name="lru_prep",
    )(p, p, p, p, conv_w, conv_b.reshape(1, w), w_gate_bf16, b_gate.reshape(2, 2, 1, w), lam.reshape(2, 1, w))


def _lru_scan_kernel(a_ref, u_ref, o_ref, carry_ref, *, reverse):
    @pl.when(pl.program_id(2) == 0)
    def _():
        carry_ref[...] = jnp.zeros_like(carry_ref)

    nblk = SEQ_TILE // SUBLANES
    width = a_ref.shape[-1]
    row = lax.broadcasted_iota(jnp.int32, (SUBLANES, width), 0)

    def body(i, carry):
        blk = (nblk - 1 - i) if reverse else i
        r0 = pl.multiple_of(blk * SUBLANES, SUBLANES)
        a = a_ref[0, pl.ds(r0, SUBLANES), :]
        u = u_ref[0, pl.ds(r0, SUBLANES), :]
        for s in (1, 2, 4):
            if reverse:
                keep = row < SUBLANES - s
                shift = SUBLANES - s
            else:
                keep = row >= s
                shift = s
            a_sh = jnp.where(keep, pltpu.roll(a, shift, 0), 1.0)
            u_sh = jnp.where(keep, pltpu.roll(u, shift, 0), 0.0)
            u = a * u_sh + u
            a = a * a_sh
        h = u + a * carry
        o_ref[pl.ds(r0, SUBLANES), :] = h
        return h[0:1, :] if reverse else h[SUBLANES - 1:SUBLANES, :]

    carry_ref[...] = lax.fori_loop(0, nblk, body, carry_ref[...])


def lru_scan(a, u, direction):
    reverse = direction == 1
    tw = 512

    def row_tile(b, k):
        lat = b * LAT_TILES + ((LAT_TILES - k) if reverse else (k - 1))
        return jnp.where(k == 0, BATCH * LAT_TILES + b, lat)

    spec3 = pl.BlockSpec((1, SEQ_TILE, tw), lambda b, l, k: (direction, row_tile(b, k), l))
    return pl.pallas_call(
        functools.partial(_lru_scan_kernel, reverse=reverse),
        grid=(BATCH, LRU_W // tw, LAT_TILES + 1),
        in_specs=[spec3, spec3],
        out_specs=pl.BlockSpec((SEQ_TILE, tw), lambda b, l, k: (row_tile(b, k), l)),
        out_shape=jax.ShapeDtypeStruct((T_ROWS, LRU_W), F32),
        scratch_shapes=[pltpu.VMEM((1, tw), F32)],
        compiler_params=_params("arbitrary", "arbitrary", "arbitrary"),
        name="lru_scan_bwd" if reverse else "lru_scan_fwd",
    )(a, u)


def _lru_post_kernel(hf_ref, hb_ref, y_ref, o_ref):
    o_ref[...] = ((hf_ref[...] + hb_ref[...]) * y_ref[...]).astype(BF16)


def lru_post(hf, hb, y):
    tm = 512
    spec = pl.BlockSpec((tm, LRU_W), lambda t: (t, 0))
    return pl.pallas_call(
        _lru_post_kernel,
        grid=(T_ROWS // tm,),
        in_specs=[spec, spec, spec],
        out_specs=spec,
        out_shape=jax.ShapeDtypeStruct((T_ROWS, LRU_W), BF16),
        compiler_params=_params("arbitrary"),
        name="lru_post",
    )(hf, hb, y)


def lru_mixer(p, conv_w, conv_b, w_gate, b_gate, lam):
    y, a, u = lru_prep(p, conv_w, conv_b, w_gate.astype(BF16), b_gate, lam)
    return lru_post(lru_scan(a, u, 0), lru_scan(a, u, 1), y)


def _split_stream(t):
    return t[:N_LAT].reshape(BATCH, SEQ, -1), t[N_LAT:].reshape(BATCH, CTX_LEN, -1)


def _join_stream(lat, ctx):
    return jnp.concatenate([lat.reshape(N_LAT, -1), ctx.reshape(N_CTX, -1)], axis=0)


def _rmsnorm(x, w):
    return x * lax.rsqrt(jnp.mean(x * x, axis=-1, keepdims=True) + EPS) * w


def _l2norm(t):
    return t * lax.rsqrt(jnp.sum(t * t, axis=-1, keepdims=True) + EPS)


def _dwconv(x, w):
    K, C = w.shape
    left = K // 2
    return lax.conv_general_dilated(x, w[:, None, :], window_strides=(1,), padding=[(left, K - 1 - left)],
                                    dimension_numbers=('NWC', 'WIO', 'NWC'), feature_group_count=C)


def _split_heads(t, n):
    B, S, _ = t.shape
    return t.reshape(B, S, n, -1).transpose(0, 2, 1, 3)


def _to_chunks(t):
    B, H, S = t.shape[:3]
    t = t.reshape(B, H, S // CHUNK, CHUNK, *t.shape[3:])
    return jnp.moveaxis(t, 2, 0)


def _from_chunks(t):
    t = jnp.moveaxis(t, 0, 2)
    return t.reshape(t.shape[0], t.shape[1], -1, *t.shape[4:])


def _bidirectional(scan_fn, ctx_f, lat_f, ctx_b, lat_b, init, axis):
    rev = lambda ts: tuple(jnp.flip(t, axis) for t in ts)
    yc_f, st_f = scan_fn(*ctx_f, init)
    yl_f, _ = scan_fn(*lat_f, st_f)
    yc_b, st_b = scan_fn(*rev(ctx_b), init)
    yl_b, _ = scan_fn(*rev(lat_b), st_b)
    return yc_f + jnp.flip(yc_b, axis), yl_f + jnp.flip(yl_b, axis)


def _gdn_chunked(q, k, v, g, beta, state0):
    q, k, v, g, beta = map(_to_chunks, (q, k, v, g, beta))
    L = CHUNK
    incl = jnp.tril(jnp.ones((L, L), bool))
    strict = jnp.tril(jnp.ones((L, L), bool), -1)
    g = jnp.cumsum(g, axis=-1)
    decay = jnp.exp(jnp.where(incl, g[..., :, None] - g[..., None, :], -jnp.inf))
    kb = k * beta[..., None]
    A = jnp.where(strict, jnp.einsum('nbhid,nbhjd->nbhij', kb, k) * decay, 0.0) + jnp.eye(L, dtype=F32)
    u = lax.linalg.triangular_solve(A, v * beta[..., None], left_side=True, lower=True, unit_diagonal=True)
    w = lax.linalg.triangular_solve(A, kb * jnp.exp(g)[..., None], left_side=True, lower=True, unit_diagonal=True)
    a_qk = jnp.einsum('nbhid,nbhjd->nbhij', q, k) * decay

    def step(S, xs):
        qc, kc, uc, wc, gc, ac = xs
        v_new = uc - wc @ S
        o = (qc * jnp.exp(gc)[..., None]) @ S + ac @ v_new
        gl = gc[..., -1:]
        S = S * jnp.exp(gl)[..., None] + jnp.einsum('bhld,bhle->bhde', kc * jnp.exp(gl - gc)[..., None], v_new)
        return S, o

    S_fin, o = lax.scan(step, state0, (q, k, u, w, g, a_qk))
    return _from_chunks(o), S_fin


def _gdn_core_jax(p, conv_w, a_log, dt_bias, norm_w):
    def prep(pp):
        B, S, _ = pp.shape
        qkv = jax.nn.silu(_dwconv(pp[..., :2 * GDN_QK + GDN_VW], conv_w))
        z = pp[..., 2 * GDN_QK + GDN_VW:2 * GDN_QK + 2 * GDN_VW].reshape(B, S, GDN_V_HEADS, GDN_DV)
        q = jnp.repeat(_l2norm(_split_heads(qkv[..., :GDN_QK], GDN_QK_HEADS)), GDN_REP, axis=1) * GDN_DK ** -0.5
        k = jnp.repeat(_l2norm(_split_heads(qkv[..., GDN_QK:2 * GDN_QK], GDN_QK_HEADS)), GDN_REP, axis=1)
        v = _split_heads(qkv[..., 2 * GDN_QK:], GDN_V_HEADS)
        ba = pp[..., 2 * GDN_QK + 2 * GDN_VW:GDN_IN].reshape(B, S, 2, 2, GDN_V_HEADS)
        beta = jax.nn.sigmoid(ba[:, :, 0]).transpose(0, 2, 3, 1)
        g = -jnp.exp(a_log)[:, :, None] * jax.nn.softplus(ba[:, :, 1].transpose(0, 2, 3, 1) + dt_bias[:, :, None])
        return q, k, v, beta, g, z

    pl_, pc = _split_stream(p)
    qc, kc, vc, bc, gc, zc = prep(pc)
    ql, kl, vl, bl, gl, zl = prep(pl_)
    init = jnp.zeros((BATCH, GDN_V_HEADS, GDN_DK, GDN_DV), F32)
    oc, ol = _bidirectional(_gdn_chunked,
                            (qc, kc, vc, gc[:, 0], bc[:, 0]), (ql, kl, vl, gl[:, 0], bl[:, 0]),
                            (qc, kc, vc, gc[:, 1], bc[:, 1]), (ql, kl, vl, gl[:, 1], bl[:, 1]),
                            init, axis=2)

    def out(o, z):
        y = _rmsnorm(o.transpose(0, 2, 1, 3), norm_w) * jax.nn.silu(z)
        return y.reshape(y.shape[0], y.shape[1], GDN_VW)

    return _join_stream(out(ol, zl), out(oc, zc)).astype(BF16)


def _mlstm_chunked(q, k, v, ig, lf, state):
    L = CHUNK
    causal = jnp.tril(jnp.ones((L, L), bool))

    def step(carry, xs):
        C, n, m = carry
        qc, kc, vc, ic, fc = xs
        b = jnp.cumsum(fc, axis=-1)
        d_log = jnp.where(causal, b[..., :, None] - b[..., None, :] + ic[..., None, :], -jnp.inf)
        inter = b + m[..., None]
        m_t = jnp.maximum(inter, jnp.max(d_log, axis=-1))
        s = jnp.einsum('bhld,bhsd->bhls', qc, kc) * jnp.exp(d_log - m_t[..., None])
        w_inter = jnp.exp(inter - m_t)
        num = w_inter[..., None] * jnp.einsum('bhld,bhde->bhle', qc, C) + jnp.einsum('bhls,bhse->bhle', s, vc)
        den = w_inter * jnp.einsum('bhld,bhd->bhl', qc, n) + jnp.sum(s, axis=-1)
        h = num / jnp.maximum(jnp.abs(den), jnp.exp(-m_t))[..., None]
        b_last = b[..., -1]
        w_log = b_last[..., None] - b + ic
        m_new = jnp.maximum(b_last + m, jnp.max(w_log, axis=-1))
        w_k = jnp.exp(w_log - m_new[..., None])
        carry_decay = jnp.exp(b_last + m - m_new)
        C = carry_decay[..., None, None] * C + jnp.einsum('bhl,bhld,bhle->bhde', w_k, kc, vc)
        n = carry_decay[..., None] * n + jnp.einsum('bhl,bhld->bhd', w_k, kc)
        return (C, n, m_new), h

    state, h = lax.scan(step, state, tuple(map(_to_chunks, (q, k, v, ig, lf))))
    return _from_chunks(h), state


def _mlstm_core_jax(p, gate_b, norm_w):
    def prep(pp):
        B, S, _ = pp.shape
        q = _split_heads(pp[..., :ML_QK], ML_HEADS) * ML_DQK ** -0.5
        k = _split_heads(pp[..., ML_QK:2 * ML_QK], ML_HEADS)
        v = _split_heads(pp[..., 2 * ML_QK:2 * ML_QK + ML_V], ML_HEADS)
        o = pp[..., 2 * ML_QK + ML_V:2 * ML_QK + 2 * ML_V].reshape(B, S, ML_HEADS, ML_DV)
        gt = pp[..., 2 * ML_QK + 2 * ML_V:ML_IN].reshape(B, S, 2, 2, ML_HEADS) + gate_b
        gt = (GATE_CAP * jnp.tanh(gt / GATE_CAP)).transpose(0, 2, 3, 4, 1)
        return q, k, v, gt[:, :, 0], jax.nn.log_sigmoid(gt[:, :, 1]), o

    pl_, pc = _split_stream(p)
    qc, kc, vc, ic, fc, oc = prep(pc)
    ql, kl, vl, il, fl, ol = prep(pl_)
    init = (jnp.zeros((BATCH, ML_HEADS, ML_DQK, ML_DV), F32), jnp.zeros((BATCH, ML_HEADS, ML_DQK), F32),
            jnp.zeros((BATCH, ML_HEADS), F32))
    hc, hl = _bidirectional(_mlstm_chunked,
                            (qc, kc, vc, ic[:, 0], fc[:, 0]), (ql, kl, vl, il[:, 0], fl[:, 0]),
                            (qc, kc, vc, ic[:, 1], fc[:, 1]), (ql, kl, vl, il[:, 1], fl[:, 1]),
                            init, axis=2)

    def out(hh, o):
        y = _rmsnorm(hh.transpose(0, 2, 1, 3), norm_w.reshape(ML_HEADS, ML_DV)) * jax.nn.sigmoid(o)
        return y.reshape(y.shape[0], y.shape[1], ML_V)

    return _join_stream(out(hl, ol), out(hc, oc)).astype(BF16)


def _pad_cols(w, mult):
    n = w.shape[1]
    return jnp.pad(w, ((0, 0), (0, (-n) % mult)))


def _lat_transpose(r, rows, cols):
    lat = r[:N_LAT].reshape(BATCH, rows, cols, D_MODEL).transpose(0, 2, 1, 3).reshape(N_LAT, D_MODEL)
    return jnp.concatenate([lat, r[N_LAT:]], axis=0)


def kernel(x, c, ctx, c_ctx, w_mod, b_mod, norm_mix, norm_ff, norm_out, gdn_w_in, gdn_conv, gdn_a_log, gdn_dt_bias, gdn_norm, gdn_w_out, ml_w_in, ml_gate_b, ml_norm, ml_w_out, lru_w_in, lru_conv, lru_conv_b, lru_w_gate, lru_b_gate, lru_lambda, lru_w_out, ff_w_up, ff_w_down):
    r = jnp.concatenate([x.reshape(N_LAT, D_MODEL), ctx.reshape(N_CTX, D_MODEL)], axis=0)
    c_all = jnp.concatenate([c, c_ctx[None], jnp.zeros((MOD_ROWS - BATCH - 1, D_MODEL), F32)], axis=0)
    mod = modulation_table(c_all, w_mod, b_mod)
    rows = SEQ // GRID_W
    col_order = False
    for i in range(DEPTH):
        kind, j = i % N_MIXERS, i // N_MIXERS
        col = i % 2 == 1
        if col != col_order:
            r = _lat_transpose(r, rows, GRID_W) if col else _lat_transpose(r, GRID_W, rows)
            col_order = col
        if kind == 0:
            p = in_proj(r, norm_mix[i], mod[i], _pad_cols(gdn_w_in[j], COL_TILE).astype(BF16))
            y = _gdn_core_jax(p, gdn_conv[j], gdn_a_log[j], gdn_dt_bias[j], gdn_norm[j])
            w_out = gdn_w_out[j]
        elif kind == 1:
            p = in_proj(r, norm_mix[i], mod[i], _pad_cols(ml_w_in[j], COL_TILE).astype(BF16))
            y = _mlstm_core_jax(p, ml_gate_b[j], ml_norm[j])
            w_out = ml_w_out[j]
        else:
            p = in_proj(r, norm_mix[i], mod[i], lru_w_in[j].astype(BF16))
            y = lru_mixer(p, lru_conv[j], lru_conv_b[j], lru_w_gate[j], lru_b_gate[j], lru_lambda[j])
            w_out = lru_w_out[j]
        r = out_proj(y, w_out.astype(BF16), r, mod[i])
        r = mlp(r, norm_ff[i], mod[i], ff_w_up[i].astype(BF16), ff_w_down[i].astype(BF16))
    out = final_norm(r, norm_out).reshape(BATCH, SEQ, D_MODEL)
    if col_order:
        out = out.reshape(BATCH, GRID_W, rows, D_MODEL).transpose(0, 2, 1, 3).reshape(BATCH, SEQ, D_MODEL)
    return out
```

```python
import functools
import math

import jax
import jax.numpy as jnp
from jax import lax
from jax.experimental import pallas as pl
from jax.experimental.pallas import tpu as pltpu

F32 = jnp.float32
BF16 = jnp.bfloat16

D_MODEL = 2048
BATCH = 4
SEQ = 4096
DEPTH = 4
CTX_LEN = 256
GRID_W = 64
N_MIXERS = 3
CHUNK = 64
CONV_W = 4
EPS = 1e-6
D_FF = 4 * D_MODEL

GDN_QK_HEADS = D_MODEL // 128
GDN_V_HEADS = 2 * GDN_QK_HEADS
GDN_DK = 128
GDN_DV = 128
GDN_QK = GDN_QK_HEADS * GDN_DK
GDN_VW = GDN_V_HEADS * GDN_DV
GDN_REP = GDN_V_HEADS // GDN_QK_HEADS
GDN_IN = 2 * GDN_QK + 2 * GDN_VW + 4 * GDN_V_HEADS

ML_HEADS = 8
ML_DQK = D_MODEL // (2 * ML_HEADS)
ML_DV = D_MODEL // ML_HEADS
ML_QK = ML_HEADS * ML_DQK
ML_V = ML_HEADS * ML_DV
ML_IN = 2 * ML_QK + 2 * ML_V + 4 * ML_HEADS
GATE_CAP = 15.0

LRU_W = D_MODEL
LRU_BLOCKS = 8
LRU_BW = LRU_W // LRU_BLOCKS
LRU_C = 8.0

N_LAT = BATCH * SEQ
N_CTX = BATCH * CTX_LEN
T_ROWS = N_LAT + N_CTX
MOD_ROWS = 8
CTX_MOD_ROW = BATCH

V7X_VMEM_BYTES = 64 * 1024 * 1024
VMEM_LIMIT = 56 * 1024 * 1024
SUBLANES = 8
LANES = 128

ROW_TILE = 1024
MLP_ROW_TILE = 512
COL_TILE = 512
SEQ_TILE = 256


def _params(*sem):
    return pltpu.CompilerParams(dimension_semantics=sem, vmem_limit_bytes=VMEM_LIMIT)


def _mod_row(tile, tile_rows):
    lat_tiles = N_LAT // tile_rows
    return jnp.where(tile < lat_tiles, tile // (SEQ // tile_rows), CTX_MOD_ROW)


def _mod_kernel(c_ref, w_ref, b_ref, o_ref):
    c = c_ref[...]
    s = c * jax.nn.sigmoid(c)
    o_ref[0] = jnp.dot(s.astype(BF16), w_ref[0].astype(BF16), preferred_element_type=F32) + b_ref[0]


def modulation_table(c_all, w_mod, b_mod):
    tn = 1024
    n6 = 6 * D_MODEL
    return pl.pallas_call(
        _mod_kernel,
        grid=(DEPTH, n6 // tn),
        in_specs=[pl.BlockSpec((MOD_ROWS, D_MODEL), lambda i, n: (0, 0)),
                  pl.BlockSpec((1, D_MODEL, tn), lambda i, n: (i, 0, n)),
                  pl.BlockSpec((1, 1, tn), lambda i, n: (i, 0, n))],
        out_specs=pl.BlockSpec((1, MOD_ROWS, tn), lambda i, n: (i, 0, n)),
        out_shape=jax.ShapeDtypeStruct((DEPTH, MOD_ROWS, n6), F32),
        compiler_params=_params("arbitrary", "arbitrary"),
        name="modulation_table",
    )(c_all, w_mod, b_mod.reshape(DEPTH, 1, n6))


def _norm_modulate(x, nw, shift, scale):
    y = x * lax.rsqrt(jnp.mean(x * x, axis=-1, keepdims=True) + EPS)
    return (y * nw) * (1.0 + scale) + shift


def _in_proj_kernel(x_ref, nw_ref, sh_ref, sc_ref, w_ref, o_ref, h_ref, *, tm):
    row = _mod_row(pl.program_id(0), tm)

    @pl.when(pl.program_id(1) == 0)
    def _():
        h = _norm_modulate(x_ref[...], nw_ref[...], sh_ref[pl.ds(row, 1), :], sc_ref[pl.ds(row, 1), :])
        h_ref[...] = h.astype(BF16)

    o_ref[...] = jnp.dot(h_ref[...], w_ref[...], preferred_element_type=F32)


def in_proj(r, norm_w, mod_i, w_bf16):
    n = w_bf16.shape[1]
    tm, tn = ROW_TILE, COL_TILE
    return pl.pallas_call(
        functools.partial(_in_proj_kernel, tm=tm),
        grid=(T_ROWS // tm, n // tn),
        in_specs=[pl.BlockSpec((tm, D_MODEL), lambda t, j: (t, 0)),
                  pl.BlockSpec((1, D_MODEL), lambda t, j: (0, 0)),
                  pl.BlockSpec((MOD_ROWS, D_MODEL), lambda t, j: (0, 0)),
                  pl.BlockSpec((MOD_ROWS, D_MODEL), lambda t, j: (0, 1)),
                  pl.BlockSpec((D_MODEL, tn), lambda t, j: (0, j))],
        out_specs=pl.BlockSpec((tm, tn), lambda t, j: (t, j)),
        out_shape=jax.ShapeDtypeStruct((T_ROWS, n), F32),
        scratch_shapes=[pltpu.VMEM((tm, D_MODEL), BF16)],
        compiler_params=_params("arbitrary", "arbitrary"),
        name="in_proj",
    )(r, norm_w.reshape(1, D_MODEL), mod_i, mod_i, w_bf16)


def _out_proj_kernel(y_ref, w_ref, r_ref, g_ref, o_ref, *, tm):
    row = _mod_row(pl.program_id(0), tm)
    acc = jnp.dot(y_ref[...], w_ref[...], preferred_element_type=F32)
    o_ref[...] = r_ref[...] + g_ref[pl.ds(row, 1), :] * acc


def out_proj(y_bf16, w_bf16, r, mod_i):
    k = y_bf16.shape[1]
    tm, tn = ROW_TILE, COL_TILE
    nj = D_MODEL // tn
    return pl.pallas_call(
        functools.partial(_out_proj_kernel, tm=tm),
        grid=(T_ROWS // tm, nj),
        in_specs=[pl.BlockSpec((tm, k), lambda t, j: (t, 0)),
                  pl.BlockSpec((k, tn), lambda t, j: (0, j)),
                  pl.BlockSpec((tm, tn), lambda t, j: (t, j)),
                  pl.BlockSpec((MOD_ROWS, tn), lambda t, j: (0, 2 * nj + j))],
        out_specs=pl.BlockSpec((tm, tn), lambda t, j: (t, j)),
        out_shape=jax.ShapeDtypeStruct((T_ROWS, D_MODEL), F32),
        compiler_params=_params("arbitrary", "arbitrary"),
        name="out_proj",
    )(y_bf16, w_bf16, r, mod_i)


def _mlp_kernel(x_ref, nw_ref, sh_ref, sc_ref, g_ref, wu_ref, wd_ref, o_ref, h_ref, acc_ref, *, tm):
    f = pl.program_id(1)
    row = _mod_row(pl.program_id(0), tm)

    @pl.when(f == 0)
    def _():
        h = _norm_modulate(x_ref[...], nw_ref[...], sh_ref[pl.ds(row, 1), :], sc_ref[pl.ds(row, 1), :])
        h_ref[...] = h.astype(BF16)
        acc_ref[...] = jnp.zeros_like(acc_ref)

    u = jnp.dot(h_ref[...], wu_ref[...], preferred_element_type=F32)
    a = jnp.square(jnp.maximum(u, 0.0))
    acc_ref[...] += jnp.dot(a.astype(BF16), wd_ref[...], preferred_element_type=F32)

    @pl.when(f == pl.num_programs(1) - 1)
    def _():
        o_ref[...] = x_ref[...] + g_ref[pl.ds(row, 1), :] * acc_ref[...]


def mlp(r, norm_w, mod_i, wu_bf16, wd_bf16):
    tm, tf = MLP_ROW_TILE, 1024
    return pl.pallas_call(
        functools.partial(_mlp_kernel, tm=tm),
        grid=(T_ROWS // tm, D_FF // tf),
        in_specs=[pl.BlockSpec((tm, D_MODEL), lambda t, f: (t, 0)),
                  pl.BlockSpec((1, D_MODEL), lambda t, f: (0, 0)),
                  pl.BlockSpec((MOD_ROWS, D_MODEL), lambda t, f: (0, 3)),
                  pl.BlockSpec((MOD_ROWS, D_MODEL), lambda t, f: (0, 4)),
                  pl.BlockSpec((MOD_ROWS, D_MODEL), lambda t, f: (0, 5)),
                  pl.BlockSpec((D_MODEL, tf), lambda t, f: (0, f)),
                  pl.BlockSpec((tf, D_MODEL), lambda t, f: (f, 0))],
        out_specs=pl.BlockSpec((tm, D_MODEL), lambda t, f: (t, 0)),
        out_shape=jax.ShapeDtypeStruct((T_ROWS, D_MODEL), F32),
        scratch_shapes=[pltpu.VMEM((tm, D_MODEL), BF16), pltpu.VMEM((tm, D_MODEL), F32)],
        compiler_params=_params("arbitrary", "arbitrary"),
        name="mlp",
    )(r, norm_w.reshape(1, D_MODEL), mod_i, mod_i, mod_i, wu_bf16, wd_bf16)


def _final_norm_kernel(x_ref, w_ref, o_ref):
    x = x_ref[...]
    o_ref[...] = x * lax.rsqrt(jnp.mean(x * x, axis=-1, keepdims=True) + EPS) * w_ref[...]


def final_norm(r, w):
    tm = 512
    return pl.pallas_call(
        _final_norm_kernel,
        grid=(N_LAT // tm,),
        in_specs=[pl.BlockSpec((tm, D_MODEL), lambda t: (t, 0)),
                  pl.BlockSpec((1, D_MODEL), lambda t: (0, 0))],
        out_specs=pl.BlockSpec((tm, D_MODEL), lambda t: (t, 0)),
        out_shape=jax.ShapeDtypeStruct((N_LAT, D_MODEL), F32),
        compiler_params=_params("arbitrary"),
        name="final_norm",
    )(r, w.reshape(1, D_MODEL))


LAT_TILES = SEQ // SEQ_TILE
N_SEQ_TILES = T_ROWS // SEQ_TILE
HALO_BLOCKS = SEQ_TILE // SUBLANES


def _seg_first(t):
    return jnp.logical_or(t >= BATCH * LAT_TILES, t % LAT_TILES == 0)


def _seg_last(t):
    return jnp.logical_or(t >= BATCH * LAT_TILES, t % LAT_TILES == LAT_TILES - 1)


def _conv4(ext_ref, cw_ref, tm):
    acc = cw_ref[0:1, :] * ext_ref[pl.ds(SUBLANES - 2, tm), :]
    acc = acc + cw_ref[1:2, :] * ext_ref[pl.ds(SUBLANES - 1, tm), :]
    acc = acc + cw_ref[2:3, :] * ext_ref[pl.ds(SUBLANES, tm), :]
    acc = acc + cw_ref[3:4, :] * ext_ref[pl.ds(SUBLANES + 1, tm), :]
    return acc


def _fill_halo(ext_ref, prev_ref, cur_ref, next_ref, t, tm):
    zero = jnp.zeros((SUBLANES, ext_ref.shape[1]), F32)
    ext_ref[pl.ds(0, SUBLANES), :] = jnp.where(_seg_first(t), zero, prev_ref[...])
    ext_ref[pl.ds(SUBLANES, tm), :] = cur_ref[...]
    ext_ref[pl.ds(SUBLANES + tm, SUBLANES), :] = jnp.where(_seg_last(t), zero, next_ref[...])


def _halo_specs(width, col_of):
    last_blk = T_ROWS // SUBLANES - 1
    return [pl.BlockSpec((SUBLANES, width), lambda t, *g: (jnp.maximum(t * HALO_BLOCKS - 1, 0), col_of(t, *g))),
            pl.BlockSpec((SEQ_TILE, width), lambda t, *g: (t, col_of(t, *g))),
            pl.BlockSpec((SUBLANES, width),
                         lambda t, *g: (jnp.minimum((t + 1) * HALO_BLOCKS, last_blk), col_of(t, *g)))]


def _lru_prep_kernel(py_ref, xp_ref, xc_ref, xn_ref, cw_ref, cb_ref, wg_ref, bg_ref, lam_ref,
                     y_ref, a_ref, u_ref, ext_ref):
    t = pl.program_id(0)
    tm = SEQ_TILE
    y_ref[...] = jax.nn.gelu(py_ref[...])
    _fill_halo(ext_ref, xp_ref, xc_ref, xn_ref, t, tm)
    xr = _conv4(ext_ref, cw_ref, tm) + cb_ref[...]
    xb = xr.astype(BF16)
    for d in range(2):
        gts = []
        for g in range(2):
            cols = []
            for n in range(LRU_BLOCKS):
                cols.append(jnp.dot(xb[:, n * LRU_BW:(n + 1) * LRU_BW], wg_ref[d, g, n],
                                    preferred_element_type=F32))
            gts.append(jnp.concatenate(cols, axis=1) + bg_ref[d, g])
        log_a = -LRU_C * jax.nn.sigmoid(gts[0]) * jax.nn.softplus(-lam_ref[d])
        a = jnp.exp(log_a)
        a_ref[d] = a
        one_minus_a2 = -jnp.tanh(log_a) * (a * a + 1.0)
        u_ref[d] = jnp.sqrt(one_minus_a2) * jax.nn.sigmoid(gts[1]) * xr


def lru_prep(p, conv_w, conv_b, w_gate_bf16, b_gate, lam):
    w = LRU_W
    full = lambda shape: pl.BlockSpec(shape, lambda t: (0,) * len(shape))
    return pl.pallas_call(
        _lru_prep_kernel,
        grid=(N_SEQ_TILES,),
        in_specs=[pl.BlockSpec((SEQ_TILE, w), lambda t: (t, 0))] + _halo_specs(w, lambda t: 1) + [
            full((CONV_W, w)), full((1, w)),
            full((2, 2, LRU_BLOCKS, LRU_BW, LRU_BW)), full((2, 2, 1, w)), full((2, 1, w))],
        out_specs=[pl.BlockSpec((SEQ_TILE, w), lambda t: (t, 0)),
                   pl.BlockSpec((2, SEQ_TILE, w), lambda t: (0, t, 0)),
                   pl.BlockSpec((2, SEQ_TILE, w), lambda t: (0, t, 0))],
        out_shape=[jax.ShapeDtypeStruct((T_ROWS, w), F32),
                   jax.ShapeDtypeStruct((2, T_ROWS, w), F32),
                   jax.ShapeDtypeStruct((2, T_ROWS, w), F32)],
        scratch_shapes=[pltpu.VMEM((SEQ_TILE + 2 * SUBLANES, w), F32)],
        compiler_params=_params("arbitrary"),
        name="lru_prep",
    )(p, p, p, p, conv_w, conv_b.reshape(1, w), w_gate_bf16, b_gate.reshape(2, 2, 1, w), lam.reshape(2, 1, w))


def _lru_scan_kernel(a_ref, u_ref, o_ref, carry_ref, *, reverse):
    @pl.when(pl.program_id(2) == 0)
    def _():
        carry_ref[...] = jnp.zeros_like(carry_ref)

    nblk = SEQ_TILE // SUBLANES
    width = a_ref.shape[-1]
    row = lax.broadcasted_iota(jnp.int32, (SUBLANES, width), 0)

    def body(i, carry):
        blk = (nblk - 1 - i) if reverse else i
        r0 = pl.multiple_of(blk * SUBLANES, SUBLANES)
        a = a_ref[0, pl.ds(r0, SUBLANES), :]
        u = u_ref[0, pl.ds(r0, SUBLANES), :]
        for s in (1, 2, 4):
            if reverse:
                keep = row < SUBLANES - s
                shift = SUBLANES - s
            else:
                keep = row >= s
                shift = s
            a_sh = jnp.where(keep, pltpu.roll(a, shift, 0), 1.0)
            u_sh = jnp.where(keep, pltpu.roll(u, shift, 0), 0.0)
            u = a * u_sh + u
            a = a * a_sh
        h = u + a * carry
        o_ref[pl.ds(r0, SUBLANES), :] = h
        return h[0:1, :] if reverse else h[SUBLANES - 1:SUBLANES, :]

    carry_ref[...] = lax.fori_loop(0, nblk, body, carry_ref[...])


def lru_scan(a, u, direction):
    reverse = direction == 1
    tw = 512

    def row_tile(b, k):
        lat = b * LAT_TILES + ((LAT_TILES - k) if reverse else (k - 1))
        return jnp.where(k == 0, BATCH * LAT_TILES + b, lat)

    spec3 = pl.BlockSpec((1, SEQ_TILE, tw), lambda b, l, k: (direction, row_tile(b, k), l))
    return pl.pallas_call(
        functools.partial(_lru_scan_kernel, reverse=reverse),
        grid=(BATCH, LRU_W // tw, LAT_TILES + 1),
        in_specs=[spec3, spec3],
        out_specs=pl.BlockSpec((SEQ_TILE, tw), lambda b, l, k: (row_tile(b, k), l)),
        out_shape=jax.ShapeDtypeStruct((T_ROWS, LRU_W), F32),
        scratch_shapes=[pltpu.VMEM((1, tw), F32)],
        compiler_params=_params("arbitrary", "arbitrary", "arbitrary"),
        name="lru_scan_bwd" if reverse else "lru_scan_fwd",
    )(a, u)


def _lru_post_kernel(hf_ref, hb_ref, y_ref, o_ref):
    o_ref[...] = ((hf_ref[...] + hb_ref[...]) * y_ref[...]).astype(BF16)


def lru_post(hf, hb, y):
    tm = 512
    spec = pl.BlockSpec((tm, LRU_W), lambda t: (t, 0))
    return pl.pallas_call(
        _lru_post_kernel,
        grid=(T_ROWS // tm,),
        in_specs=[spec, spec, spec],
        out_specs=spec,
        out_shape=jax.ShapeDtypeStruct((T_ROWS, LRU_W), BF16),
        compiler_params=_params("arbitrary"),
        name="lru_post",
    )(hf, hb, y)


def lru_mixer(p, conv_w, conv_b, w_gate, b_gate, lam):
    y, a, u = lru_prep(p, conv_w, conv_b, w_gate.astype(BF16), b_gate, lam)
    return lru_post(lru_scan(a, u, 0), lru_scan(a, u, 1), y)


PAIR_SLOTS = 16
NEG_BIG = -1e30


def _split3(x):
    hi = x.astype(BF16)
    r1 = x - hi.astype(F32)
    mid = r1.astype(BF16)
    lo = (r1 - mid.astype(F32)).astype(BF16)
    return hi, mid, lo


def _gates_kernel(x_ref, pa_ref, pb_ref, o_ref, *, kind, npairs):
    x = x_ref[...]
    tm = x.shape[0]
    lane = lax.broadcasted_iota(jnp.int32, x.shape, 1)
    if kind == "gdn":
        first = jax.nn.sigmoid(x)
        second = -jnp.exp(pa_ref[...]) * jax.nn.softplus(x + pb_ref[...])
    else:
        gt = GATE_CAP * jnp.tanh((x + pb_ref[...]) / GATE_CAP)
        first = gt
        second = jax.nn.log_sigmoid(gt)
    r = lax.broadcasted_iota(jnp.int32, (tm, tm), 0)
    s = lax.broadcasted_iota(jnp.int32, (tm, tm), 1)
    same = (r // CHUNK) == (s // CHUNK)
    as_w = lambda m: jnp.where(m, 1.0, 0.0).astype(BF16)
    parts = _split3(second)

    def chunk_sum(w):
        acc = jnp.dot(w, parts[0], preferred_element_type=F32)
        acc = acc + jnp.dot(w, parts[1], preferred_element_type=F32)
        return acc + jnp.dot(w, parts[2], preferred_element_type=F32)

    prefix = chunk_sum(as_w(jnp.logical_and(same, s <= r)))
    suffix = chunk_sum(as_w(jnp.logical_and(same, s >= r)))
    total = chunk_sum(as_w(same))
    cs = jnp.where((lane & 3) >= 2, suffix, prefix)
    x1 = jnp.where((lane & 7) < 4, first, cs)
    for j in range(npairs):
        o_ref[j] = jnp.concatenate([x1[:, 8 * j:8 * j + 8], total[:, 8 * j:8 * j + 8]], axis=1)


def gate_columns(p, col_block, row_a, row_b, kind, npairs):
    return pl.pallas_call(
        functools.partial(_gates_kernel, kind=kind, npairs=npairs),
        grid=(N_SEQ_TILES,),
        in_specs=[pl.BlockSpec((SEQ_TILE, LANES), lambda t: (t, col_block)),
                  pl.BlockSpec((1, LANES), lambda t: (0, 0)),
                  pl.BlockSpec((1, LANES), lambda t: (0, 0))],
        out_specs=pl.BlockSpec((npairs, SEQ_TILE, PAIR_SLOTS), lambda t: (0, t, 0)),
        out_shape=jax.ShapeDtypeStruct((npairs, T_ROWS, PAIR_SLOTS), F32),
        compiler_params=_params("arbitrary"),
        name=kind + "_gates",
    )(p, row_a, row_b)


STACK = 4 * CHUNK
CHUNKS_PER_TILE = SEQ_TILE // CHUNK


def _mm(a, b):
    return jnp.dot(a.astype(BF16), b.astype(BF16), preferred_element_type=F32)


def _mm_nt(a, b):
    return lax.dot_general(a.astype(BF16), b.astype(BF16), (((1,), (1,)), ((), ())),
                           preferred_element_type=F32)


def _stack_masks():
    r = lax.broadcasted_iota(jnp.int32, (STACK, STACK), 0)
    s = lax.broadcasted_iota(jnp.int32, (STACK, STACK), 1)
    same = (r // CHUNK) == (s // CHUNK)
    fwd = r < 2 * CHUNK
    ordered = jnp.logical_or(jnp.logical_and(fwd, s <= r), jnp.logical_and(jnp.logical_not(fwd), s >= r))
    incl = jnp.logical_and(same, ordered)
    strict = jnp.logical_and(incl, r != s)
    return r, s, incl, strict


def _col4(gf, gb, slot):
    return jnp.concatenate([gf[:, slot:slot + 1], gf[:, slot + 1:slot + 2],
                            gb[:, slot + 2:slot + 3], gb[:, slot + 3:slot + 4]], axis=0)


def _diag_blocks(x, rows0, width):
    return jnp.concatenate([x[rows0 + c * CHUNK:rows0 + (c + 1) * CHUNK, c * width:(c + 1) * width]
                            for c in range(4)], axis=0)


def _block_diag_place(v, r_col):
    chain = r_col // CHUNK
    return jnp.concatenate([jnp.where(chain == c, v, 0.0) for c in range(4)], axis=1)


def _chain_scale(col, width):
    blocks = []
    for c in range(4):
        t = jnp.broadcast_to(col[c * CHUNK:(c + 1) * CHUNK], (CHUNK, width))
        blocks.append(jnp.concatenate([t, t], axis=0))
    return jnp.concatenate(blocks, axis=1)


def _scan_tile(b, k, reverse):
    lat = b * LAT_TILES + ((LAT_TILES - k) if reverse else (k - 1))
    return jnp.where(k == 0, BATCH * LAT_TILES + b, lat)


def _gdn_conv_kernel(xp_ref, xc_ref, xn_ref, cw_ref, o_ref, ext_ref):
    t = pl.program_id(0)
    ct = pl.program_id(1)
    _fill_halo(ext_ref, xp_ref, xc_ref, xn_ref, t, SEQ_TILE)
    y = _conv4(ext_ref, cw_ref, SEQ_TILE)
    y = y * jax.nn.sigmoid(y)
    qk_tiles = GDN_QK // COL_TILE
    is_q = ct < qk_tiles
    is_qk = ct < 2 * qk_tiles
    for h in range(COL_TILE // GDN_DK):
        blk = y[:, h * GDN_DK:(h + 1) * GDN_DK]
        nrm = blk * lax.rsqrt(jnp.sum(blk * blk, axis=-1, keepdims=True) + EPS)
        nrm = jnp.where(is_q, nrm * GDN_DK ** -0.5, nrm)
        o_ref[:, h * GDN_DK:(h + 1) * GDN_DK] = jnp.where(is_qk, nrm, blk)


def gdn_conv(p, conv_w):
    n = 2 * GDN_QK + GDN_VW
    return pl.pallas_call(
        _gdn_conv_kernel,
        grid=(N_SEQ_TILES, n // COL_TILE),
        in_specs=_halo_specs(COL_TILE, lambda t, c: c) + [pl.BlockSpec((CONV_W, COL_TILE), lambda t, c: (0, c))],
        out_specs=pl.BlockSpec((SEQ_TILE, COL_TILE), lambda t, c: (t, c)),
        out_shape=jax.ShapeDtypeStruct((T_ROWS, n), F32),
        scratch_shapes=[pltpu.VMEM((SEQ_TILE + 2 * SUBLANES, COL_TILE), F32)],
        compiler_params=_params("arbitrary", "arbitrary"),
        name="gdn_conv",
    )(p, p, p, conv_w)


def _gdn_core_kernel(qf_ref, kf_ref, vf_ref, gf_ref, qb_ref, kb_ref, vb_ref, gb_ref,
                     of_ref, ob_ref, s_ref):
    @pl.when(pl.program_id(2) == 0)
    def _():
        s_ref[...] = jnp.zeros_like(s_ref)

    r, s, incl, strict = _stack_masks()
    eye = r == s
    blk16 = (r // 16) == (s // 16)
    blk32 = (r // 32) == (s // 32)
    r_col = lax.broadcasted_iota(jnp.int32, (STACK, 1), 0)

    steps = range(CHUNKS_PER_TILE)
    rows_f = [n * CHUNK for n in steps]
    rows_b = [(CHUNKS_PER_TILE - 1 - n) * CHUNK for n in steps]
    each = lambda fn, *lists: [fn(*xs) for xs in zip(*lists)]

    def load(rf, rb):
        kf = kf_ref[pl.ds(rf, CHUNK), :]
        kb = kb_ref[pl.ds(rb, CHUNK), :]
        qf = qf_ref[pl.ds(rf, CHUNK), :]
        qb = qb_ref[pl.ds(rb, CHUNK), :]
        k4 = jnp.concatenate([kf, kf, kb, kb], axis=0)
        q4 = jnp.concatenate([qf, qf, qb, qb], axis=0)
        v4 = jnp.concatenate([vf_ref[pl.ds(rf, CHUNK), 0:GDN_DV], vf_ref[pl.ds(rf, CHUNK), GDN_DV:2 * GDN_DV],
                              vb_ref[pl.ds(rb, CHUNK), 0:GDN_DV], vb_ref[pl.ds(rb, CHUNK), GDN_DV:2 * GDN_DV]],
                             axis=0)
        gf = gf_ref[0, pl.ds(rf, CHUNK), :]
        gb = gb_ref[0, pl.ds(rb, CHUNK), :]
        return k4, q4, v4, _col4(gf, gb, 0), _col4(gf, gb, 4), _col4(gf, gb, 12)

    k4, q4, v4, beta, gc, gl = zip(*each(load, rows_f, rows_b))
    eg = each(jnp.exp, gc)
    gram = each(lambda k, q: _mm_nt(jnp.concatenate([k, q], axis=0), k), k4, q4)

    def decay_of(g):
        g_b = jnp.broadcast_to(g, (STACK, STACK))
        return jnp.exp(jnp.where(incl, g_b - g_b.T, NEG_BIG))

    decay = each(decay_of, gc)
    nmat = each(lambda gm, d, b: jnp.where(strict, gm[:STACK] * d * b, 0.0), gram, decay, beta)
    a_qk = each(lambda gm, d: gm[STACK:] * d, gram, decay)

    n0 = each(lambda x: jnp.where(blk16, x, 0.0), nmat)
    p2 = each(_mm, n0, n0)
    p4 = each(_mm, p2, p2)
    t = each(lambda x: jnp.where(eye, 1.0, 0.0) - x, n0)
    t = each(lambda a, b: a + _mm(a, b), t, p2)
    p8 = each(_mm, p4, p4)
    t = each(lambda a, b: a + _mm(a, b), t, p4)
    t = each(lambda a, b: a + _mm(a, b), t, p8)
    for level_mask in (jnp.logical_and(blk32, jnp.logical_not(blk16)), jnp.logical_not(blk32)):
        tn = each(lambda a, x: _mm(a, jnp.where(level_mask, x, 0.0)), t, nmat)
        t = each(lambda a, b: a - _mm(b, a), t, tn)
    uw = each(lambda a, v, k, b, e: _mm(a, jnp.concatenate([v * b, k * (b * e)], axis=1)),
              t, v4, k4, beta, eg)
    wq = each(lambda x, q, e: jnp.concatenate([x[:, GDN_DV:], q * e], axis=0), uw, q4, eg)
    k_t = each(lambda k, g, gt: (k * jnp.exp(gt - g)).T, k4, gc, gl)
    s_scale = each(lambda gt: _chain_scale(jnp.exp(gt), GDN_DV), gl)

    s_all = s_ref[...]
    for n in steps:
        xs = _mm(wq[n], s_all)
        v_new = uw[n][:, :GDN_DV] - _diag_blocks(xs, 0, GDN_DV)
        o4 = _diag_blocks(xs, STACK, GDN_DV) + _mm(a_qk[n], v_new)
        s_all = s_all * s_scale[n] + _mm(k_t[n], _block_diag_place(v_new, r_col))
        rf, rb = rows_f[n], rows_b[n]
        of_ref[pl.ds(rf, CHUNK), 0:GDN_DV] = o4[0:CHUNK]
        of_ref[pl.ds(rf, CHUNK), GDN_DV:2 * GDN_DV] = o4[CHUNK:2 * CHUNK]
        ob_ref[pl.ds(rb, CHUNK), 0:GDN_DV] = o4[2 * CHUNK:3 * CHUNK]
        ob_ref[pl.ds(rb, CHUNK), GDN_DV:2 * GDN_DV] = o4[3 * CHUNK:4 * CHUNK]
    s_ref[...] = s_all


def gdn_core(qkv, gcols):
    kcol = GDN_QK // GDN_DK
    vcol = 2 * GDN_QK // (2 * GDN_DV)

    def specs(reverse):
        tile = lambda b, j, k: _scan_tile(b, k, reverse)
        return [pl.BlockSpec((SEQ_TILE, GDN_DK), lambda b, j, k: (tile(b, j, k), j)),
                pl.BlockSpec((SEQ_TILE, GDN_DK), lambda b, j, k: (tile(b, j, k), kcol + j)),
                pl.BlockSpec((SEQ_TILE, 2 * GDN_DV), lambda b, j, k: (tile(b, j, k), vcol + j)),
                pl.BlockSpec((1, SEQ_TILE, PAIR_SLOTS), lambda b, j, k: (j, tile(b, j, k), 0))]

    out_spec = lambda reverse: pl.BlockSpec((SEQ_TILE, 2 * GDN_DV),
                                            lambda b, j, k: (_scan_tile(b, k, reverse), j))
    return pl.pallas_call(
        _gdn_core_kernel,
        grid=(BATCH, GDN_QK_HEADS, LAT_TILES + 1),
        in_specs=specs(False) + specs(True),
        out_specs=[out_spec(False), out_spec(True)],
        out_shape=[jax.ShapeDtypeStruct((T_ROWS, GDN_VW), F32)] * 2,
        scratch_shapes=[pltpu.VMEM((GDN_DK, 4 * GDN_DV), F32)],
        compiler_params=_params("arbitrary", "arbitrary", "arbitrary"),
        name="gdn_core",
    )(qkv, qkv, qkv, gcols, qkv, qkv, qkv, gcols)


def _head_norm_gate_kernel(of_ref, ob_ref, z_ref, nw_ref, y_ref, *, head_dim, gate):
    for h in range(of_ref.shape[1] // head_dim):
        sl = slice(h * head_dim, (h + 1) * head_dim)
        o = of_ref[:, sl] + ob_ref[:, sl]
        y = o * lax.rsqrt(jnp.mean(o * o, axis=-1, keepdims=True) + EPS) * nw_ref[:, sl]
        z = z_ref[:, sl]
        g = z * jax.nn.sigmoid(z) if gate == "silu" else jax.nn.sigmoid(z)
        y_ref[:, sl] = (y * g).astype(BF16)


def head_norm_gate(o_f, o_b, p, z_col_block, norm_row, head_dim, gate):
    width = o_f.shape[1]
    tm, tn = 512, 512
    spec = pl.BlockSpec((tm, tn), lambda t, c: (t, c))
    return pl.pallas_call(
        functools.partial(_head_norm_gate_kernel, head_dim=head_dim, gate=gate),
        grid=(T_ROWS // tm, width // tn),
        in_specs=[spec, spec, pl.BlockSpec((tm, tn), lambda t, c: (t, z_col_block + c)),
                  pl.BlockSpec((1, tn), lambda t, c: (0, c))],
        out_specs=spec,
        out_shape=jax.ShapeDtypeStruct((T_ROWS, width), BF16),
        compiler_params=_params("arbitrary", "arbitrary"),
        name="head_norm_" + gate,
    )(o_f, o_b, p, norm_row)


def gdn_mixer(p, conv_w, a_log, dt_bias, norm_w):
    zeros4 = jnp.zeros((GDN_QK_HEADS, 4), F32)
    per_pair = lambda t: t.reshape(2, GDN_QK_HEADS, 2).transpose(1, 0, 2).reshape(GDN_QK_HEADS, 4)
    row = lambda t: jnp.concatenate([zeros4, per_pair(t)], axis=1).reshape(1, LANES)
    gcols = gate_columns(p, (2 * GDN_QK + 2 * GDN_VW) // LANES, row(a_log), row(dt_bias), "gdn", GDN_QK_HEADS)
    qkv = gdn_conv(p, conv_w)
    o_f, o_b = gdn_core(qkv, gcols)
    norm_row = jnp.tile(norm_w, GDN_V_HEADS).reshape(1, GDN_VW)
    return head_norm_gate(o_f, o_b, p, (2 * GDN_QK + GDN_VW) // 512, norm_row, GDN_DV, "silu")


def gdn_in_weight(w_in):
    base = 2 * GDN_QK + 2 * GDN_VW
    gates = w_in[:, base:].reshape(D_MODEL, 2, 2, GDN_QK_HEADS, 2)
    gates = gates.transpose(0, 3, 1, 2, 4).reshape(D_MODEL, 4 * GDN_V_HEADS)
    return _pad_cols(jnp.concatenate([w_in[:, :base], gates], axis=1), COL_TILE)


ML_PAIRS = ML_HEADS // 2


def _mlstm_core_kernel(qf_ref, kf_ref, vf_ref, gf_ref, qb_ref, kb_ref, vb_ref, gb_ref,
                       of_ref, ob_ref, c_ref, n_ref, m_ref):
    @pl.when(pl.program_id(2) == 0)
    def _():
        c_ref[...] = jnp.zeros_like(c_ref)
        n_ref[...] = jnp.zeros_like(n_ref)
        m_ref[...] = jnp.zeros_like(m_ref)

    r, s, incl, _ = _stack_masks()
    r_col = lax.broadcasted_iota(jnp.int32, (STACK, 1), 0)
    last_rows = (CHUNK - 1, 2 * CHUNK - 1, 2 * CHUNK, 3 * CHUNK)

    steps = range(CHUNKS_PER_TILE)
    rows_f = [n * CHUNK for n in steps]
    rows_b = [(CHUNKS_PER_TILE - 1 - n) * CHUNK for n in steps]
    each = lambda fn, *lists: [fn(*xs) for xs in zip(*lists)]

    def load(rf, rb):
        def stack(f_ref, b_ref, w):
            return jnp.concatenate([f_ref[pl.ds(rf, CHUNK), 0:w], f_ref[pl.ds(rf, CHUNK), w:2 * w],
                                    b_ref[pl.ds(rb, CHUNK), 0:w], b_ref[pl.ds(rb, CHUNK), w:2 * w]], axis=0)

        gf = gf_ref[0, pl.ds(rf, CHUNK), :]
        gb = gb_ref[0, pl.ds(rb, CHUNK), :]
        return (stack(qf_ref, qb_ref, ML_DQK) * ML_DQK ** -0.5, stack(kf_ref, kb_ref, ML_DQK),
                stack(vf_ref, vb_ref, ML_DV), _col4(gf, gb, 0), _col4(gf, gb, 4), _col4(gf, gb, 12))

    q4, k4, v4, ig, bc, bl = zip(*each(load, rows_f, rows_b))
    d_log = each(lambda i, b: jnp.where(incl, b + jnp.broadcast_to(i - b, (STACK, STACK)).T, NEG_BIG), ig, bc)
    row_max = each(lambda d: jnp.max(d, axis=1, keepdims=True), d_log)
    w_max = each(lambda rm: jnp.concatenate([jnp.broadcast_to(rm[i:i + 1], (CHUNK, 1)) for i in last_rows], axis=0),
                 row_max)
    qk = each(_mm_nt, q4, k4)

    m_in = [m_ref[...]]
    for n in steps:
        m_in.append(jnp.maximum(bl[n] + m_in[n], w_max[n]))
    m_out = m_in[1:]
    m_ref[...] = m_in[-1]

    inter = each(lambda b, m: b + m, bc, m_in)
    m_t = each(jnp.maximum, inter, row_max)
    smat = each(lambda x, d, mt: x * jnp.exp(d - mt), qk, d_log, m_t)
    w_inter = each(lambda i, mt: jnp.exp(i - mt), inter, m_t)
    intra = each(_mm, smat, v4)
    s_sum = each(lambda x: jnp.sum(x, axis=1, keepdims=True), smat)
    cd = each(lambda b, m0, m1: jnp.exp(b + m0 - m1), bl, m_in, m_out)
    kw = each(lambda k, b, c, i, m1: k * jnp.exp(b - c + i - m1), k4, bl, bc, ig, m_out)
    d_c = each(lambda x, v: _mm(x.T, _block_diag_place(v, r_col)), kw, v4)
    c_scale = each(lambda x: _chain_scale(x, ML_DV), cd)

    c_all = c_ref[...]
    n_all = [n_ref[c:c + 1, :] for c in range(4)]
    for n in steps:
        qc = _diag_blocks(_mm(q4[n], c_all), 0, ML_DV)
        num = w_inter[n] * qc + intra[n]
        n_rows = jnp.concatenate([jnp.broadcast_to(x, (CHUNK, ML_DQK)) for x in n_all], axis=0)
        den = w_inter[n] * jnp.sum(q4[n] * n_rows, axis=1, keepdims=True) + s_sum[n]
        h4 = num / jnp.maximum(jnp.abs(den), jnp.exp(-m_t[n]))
        c_all = c_all * c_scale[n] + d_c[n]
        n_all = [cd[n][c * CHUNK:c * CHUNK + 1] * n_all[c]
                 + jnp.sum(kw[n][c * CHUNK:(c + 1) * CHUNK], axis=0, keepdims=True) for c in range(4)]
        rf, rb = rows_f[n], rows_b[n]
        of_ref[pl.ds(rf, CHUNK), 0:ML_DV] = h4[0:CHUNK]
        of_ref[pl.ds(rf, CHUNK), ML_DV:2 * ML_DV] = h4[CHUNK:2 * CHUNK]
        ob_ref[pl.ds(rb, CHUNK), 0:ML_DV] = h4[2 * CHUNK:3 * CHUNK]
        ob_ref[pl.ds(rb, CHUNK), ML_DV:2 * ML_DV] = h4[3 * CHUNK:4 * CHUNK]
    c_ref[...] = c_all
    for c in range(4):
        n_ref[c:c + 1, :] = n_all[c]


def mlstm_core(p, gcols):
    kcol = ML_QK // (2 * ML_DQK)
    vcol = 2 * ML_QK // (2 * ML_DV)

    def specs(reverse):
        tile = lambda b, j, k: _scan_tile(b, k, reverse)
        return [pl.BlockSpec((SEQ_TILE, 2 * ML_DQK), lambda b, j, k: (tile(b, j, k), j)),
                pl.BlockSpec((SEQ_TILE, 2 * ML_DQK), lambda b, j, k: (tile(b, j, k), kcol + j)),
                pl.BlockSpec((SEQ_TILE, 2 * ML_DV), lambda b, j, k: (tile(b, j, k), vcol + j)),
                pl.BlockSpec((1, SEQ_TILE, PAIR_SLOTS), lambda b, j, k: (j, tile(b, j, k), 0))]

    out_spec = lambda reverse: pl.BlockSpec((SEQ_TILE, 2 * ML_DV),
                                            lambda b, j, k: (_scan_tile(b, k, reverse), j))
    return pl.pallas_call(
        _mlstm_core_kernel,
        grid=(BATCH, ML_PAIRS, LAT_TILES + 1),
        in_specs=specs(False) + specs(True),
        out_specs=[out_spec(False), out_spec(True)],
        out_shape=[jax.ShapeDtypeStruct((T_ROWS, ML_V), F32)] * 2,
        scratch_shapes=[pltpu.VMEM((ML_DQK, 4 * ML_DV), F32), pltpu.VMEM((SUBLANES, ML_DQK), F32),
                        pltpu.VMEM((STACK, 1), F32)],
        compiler_params=_params("arbitrary", "arbitrary", "arbitrary"),
        name="mlstm_core",
    )(p, p, p, gcols, p, p, p, gcols)


def mlstm_mixer(p, gate_b, norm_w):
    bias = gate_b.reshape(2, 2, ML_PAIRS, 2).transpose(2, 1, 0, 3).reshape(1, 4 * ML_HEADS)
    bias = jnp.pad(bias, ((0, 0), (0, LANES - 4 * ML_HEADS)))
    gcols = gate_columns(p, (2 * ML_QK + 2 * ML_V) // LANES, bias, bias, "mlstm", ML_PAIRS)
    h_f, h_b = mlstm_core(p, gcols)
    return head_norm_gate(h_f, h_b, p, (2 * ML_QK + ML_V) // 512, norm_w.reshape(1, ML_V), ML_DV, "sigmoid")


def mlstm_in_weight(w_in):
    base = 2 * ML_QK + 2 * ML_V
    gates = w_in[:, base:].reshape(D_MODEL, 2, 2, ML_PAIRS, 2)
    gates = gates.transpose(0, 3, 2, 1, 4).reshape(D_MODEL, 4 * ML_HEADS)
    return _pad_cols(jnp.concatenate([w_in[:, :base], gates], axis=1), COL_TILE)


def _pad_cols(w, mult):
    n = w.shape[1]
    return jnp.pad(w, ((0, 0), (0, (-n) % mult)))


def _lat_transpose(r, rows, cols):
    lat = r[:N_LAT].reshape(BATCH, rows, cols, D_MODEL).transpose(0, 2, 1, 3).reshape(N_LAT, D_MODEL)
    return jnp.concatenate([lat, r[N_LAT:]], axis=0)


def kernel(x, c, ctx, c_ctx, w_mod, b_mod, norm_mix, norm_ff, norm_out, gdn_w_in, gdn_conv, gdn_a_log, gdn_dt_bias, gdn_norm, gdn_w_out, ml_w_in, ml_gate_b, ml_norm, ml_w_out, lru_w_in, lru_conv, lru_conv_b, lru_w_gate, lru_b_gate, lru_lambda, lru_w_out, ff_w_up, ff_w_down):
    r = jnp.concatenate([x.reshape(N_LAT, D_MODEL), ctx.reshape(N_CTX, D_MODEL)], axis=0)
    c_all = jnp.concatenate([c, c_ctx[None], jnp.zeros((MOD_ROWS - BATCH - 1, D_MODEL), F32)], axis=0)
    mod = modulation_table(c_all, w_mod, b_mod)
    rows = SEQ // GRID_W
    col_order = False
    for i in range(DEPTH):
        kind, j = i % N_MIXERS, i // N_MIXERS
        col = i % 2 == 1
        if col != col_order:
            r = _lat_transpose(r, rows, GRID_W) if col else _lat_transpose(r, GRID_W, rows)
            col_order = col
        if kind == 0:
            p = in_proj(r, norm_mix[i], mod[i], gdn_in_weight(gdn_w_in[j]).astype(BF16))
            y = gdn_mixer(p, gdn_conv[j], gdn_a_log[j], gdn_dt_bias[j], gdn_norm[j])
            w_out = gdn_w_out[j]
        elif kind == 1:
            p = in_proj(r, norm_mix[i], mod[i], mlstm_in_weight(ml_w_in[j]).astype(BF16))
            y = mlstm_mixer(p, ml_gate_b[j], ml_norm[j])
            w_out = ml_w_out[j]
        else:
            p = in_proj(r, norm_mix[i], mod[i], lru_w_in[j].astype(BF16))
            y = lru_mixer(p, lru_conv[j], lru_conv_b[j], lru_w_gate[j], lru_b_gate[j], lru_lambda[j])
            w_out = lru_w_out[j]
        r = out_proj(y, w_out.astype(BF16), r, mod[i])
        r = mlp(r, norm_ff[i], mod[i], ff_w_up[i].astype(BF16), ff_w_down[i].astype(BF16))
    out = final_norm(r, norm_out).reshape(BATCH, SEQ, D_MODEL)
    if col_order:
        out = out.reshape(BATCH, GRID_W, rows, D_MODEL).transpose(0, 2, 1, 3).reshape(BATCH, SEQ, D_MODEL)
    return out
```

```python
import functools
import math

import jax
import jax.numpy as jnp
from jax import lax
from jax.experimental import pallas as pl
from jax.experimental.pallas import tpu as pltpu

F32 = jnp.float32
BF16 = jnp.bfloat16

D_MODEL = 2048
BATCH = 4
SEQ = 4096
DEPTH = 4
CTX_LEN = 256
GRID_W = 64
N_MIXERS = 3
CHUNK = 64
CONV_W = 4
EPS = 1e-6
D_FF = 4 * D_MODEL

GDN_QK_HEADS = D_MODEL // 128
GDN_V_HEADS = 2 * GDN_QK_HEADS
GDN_DK = 128
GDN_DV = 128
GDN_QK = GDN_QK_HEADS * GDN_DK
GDN_VW = GDN_V_HEADS * GDN_DV
GDN_REP = GDN_V_HEADS // GDN_QK_HEADS
GDN_IN = 2 * GDN_QK + 2 * GDN_VW + 4 * GDN_V_HEADS

ML_HEADS = 8
ML_DQK = D_MODEL // (2 * ML_HEADS)
ML_DV = D_MODEL // ML_HEADS
ML_QK = ML_HEADS * ML_DQK
ML_V = ML_HEADS * ML_DV
ML_IN = 2 * ML_QK + 2 * ML_V + 4 * ML_HEADS
GATE_CAP = 15.0

LRU_W = D_MODEL
LRU_BLOCKS = 8
LRU_BW = LRU_W // LRU_BLOCKS
LRU_C = 8.0

N_LAT = BATCH * SEQ
N_CTX = BATCH * CTX_LEN
T_ROWS = N_LAT + N_CTX
MOD_ROWS = 8
CTX_MOD_ROW = BATCH

V7X_VMEM_BYTES = 64 * 1024 * 1024
VMEM_LIMIT = 56 * 1024 * 1024
SUBLANES = 8
LANES = 128

ROW_TILE = 1024
MLP_ROW_TILE = 512
COL_TILE = 512
SEQ_TILE = 256


def _params(*sem):
    return pltpu.CompilerParams(dimension_semantics=sem, vmem_limit_bytes=VMEM_LIMIT)


def _mod_row(tile, tile_rows):
    lat_tiles = N_LAT // tile_rows
    return jnp.where(tile < lat_tiles, tile // (SEQ // tile_rows), CTX_MOD_ROW)


def _mod_kernel(c_ref, w_ref, b_ref, o_ref):
    c = c_ref[...]
    s = c * jax.nn.sigmoid(c)
    o_ref[0] = jnp.dot(s.astype(BF16), w_ref[0].astype(BF16), preferred_element_type=F32) + b_ref[0]


def modulation_table(c_all, w_mod, b_mod):
    tn = 1024
    n6 = 6 * D_MODEL
    return pl.pallas_call(
        _mod_kernel,
        grid=(DEPTH, n6 // tn),
        in_specs=[pl.BlockSpec((MOD_ROWS, D_MODEL), lambda i, n: (0, 0)),
                  pl.BlockSpec((1, D_MODEL, tn), lambda i, n: (i, 0, n)),
                  pl.BlockSpec((1, 1, tn), lambda i, n: (i, 0, n))],
        out_specs=pl.BlockSpec((1, MOD_ROWS, tn), lambda i, n: (i, 0, n)),
        out_shape=jax.ShapeDtypeStruct((DEPTH, MOD_ROWS, n6), F32),
        compiler_params=_params("arbitrary", "arbitrary"),
        name="modulation_table",
    )(c_all, w_mod, b_mod.reshape(DEPTH, 1, n6))


def _norm_modulate(x, nw, shift, scale):
    y = x * lax.rsqrt(jnp.mean(x * x, axis=-1, keepdims=True) + EPS)
    return (y * nw) * (1.0 + scale) + shift


def _in_proj_kernel(xl_ref, xc_ref, nw_ref, sh_ref, sc_ref, w_ref, o_ref, h_ref, *, tm):
    t = pl.program_id(0)
    row = _mod_row(t, tm)
    lat_tiles = N_LAT // tm

    def prologue(x_ref):
        h = _norm_modulate(x_ref[...], nw_ref[...], sh_ref[pl.ds(row, 1), :], sc_ref[pl.ds(row, 1), :])
        h_ref[...] = h.astype(BF16)

    @pl.when(jnp.logical_and(pl.program_id(1) == 0, t < lat_tiles))
    def _():
        prologue(xl_ref)

    @pl.when(jnp.logical_and(pl.program_id(1) == 0, t >= lat_tiles))
    def _():
        prologue(xc_ref)

    o_ref[...] = jnp.dot(h_ref[...], w_ref[...], preferred_element_type=F32)


def in_proj(r_lat, r_ctx, norm_w, mod_i, w_bf16):
    n = w_bf16.shape[1]
    tm, tn = ROW_TILE, COL_TILE
    lat_tiles = N_LAT // tm
    assert N_CTX == tm
    return pl.pallas_call(
        functools.partial(_in_proj_kernel, tm=tm),
        grid=(T_ROWS // tm, n // tn),
        in_specs=[pl.BlockSpec((tm, D_MODEL), lambda t, j: (jnp.minimum(t, lat_tiles - 1), 0)),
                  pl.BlockSpec((tm, D_MODEL), lambda t, j: (0, 0)),
                  pl.BlockSpec((1, D_MODEL), lambda t, j: (0, 0)),
                  pl.BlockSpec((MOD_ROWS, D_MODEL), lambda t, j: (0, 0)),
                  pl.BlockSpec((MOD_ROWS, D_MODEL), lambda t, j: (0, 1)),
                  pl.BlockSpec((D_MODEL, tn), lambda t, j: (0, j))],
        out_specs=pl.BlockSpec((tm, tn), lambda t, j: (t, j)),
        out_shape=jax.ShapeDtypeStruct((T_ROWS, n), F32),
        scratch_shapes=[pltpu.VMEM((tm, D_MODEL), BF16)],
        compiler_params=_params("arbitrary", "arbitrary"),
        name="in_proj",
    )(r_lat, r_ctx, norm_w.reshape(1, D_MODEL), mod_i, mod_i, w_bf16)


def _segment_mod_row(tile, tile_rows, is_ctx):
    return CTX_MOD_ROW if is_ctx else tile // (SEQ // tile_rows)


def _out_proj_kernel(y_ref, w_ref, r_ref, g_ref, o_ref, *, tm, is_ctx):
    row = _segment_mod_row(pl.program_id(0), tm, is_ctx)
    acc = jnp.dot(y_ref[...], w_ref[...], preferred_element_type=F32)
    o_ref[...] = r_ref[...] + g_ref[pl.ds(row, 1), :] * acc


def out_proj(y_bf16, w_bf16, r_seg, mod_i, is_ctx):
    k = y_bf16.shape[1]
    tm, tn = ROW_TILE, COL_TILE
    nj = D_MODEL // tn
    rows = r_seg.shape[0]
    tile0 = N_LAT // tm if is_ctx else 0
    return pl.pallas_call(
        functools.partial(_out_proj_kernel, tm=tm, is_ctx=is_ctx),
        grid=(rows // tm, nj),
        in_specs=[pl.BlockSpec((tm, k), lambda t, j: (tile0 + t, 0)),
                  pl.BlockSpec((k, tn), lambda t, j: (0, j)),
                  pl.BlockSpec((tm, tn), lambda t, j: (t, j)),
                  pl.BlockSpec((MOD_ROWS, tn), lambda t, j: (0, 2 * nj + j))],
        out_specs=pl.BlockSpec((tm, tn), lambda t, j: (t, j)),
        out_shape=jax.ShapeDtypeStruct((rows, D_MODEL), F32),
        compiler_params=_params("arbitrary", "arbitrary"),
        name="out_proj_ctx" if is_ctx else "out_proj",
    )(y_bf16, w_bf16, r_seg, mod_i)


def _mlp_kernel(x_ref, nw_ref, sh_ref, sc_ref, g_ref, wu_ref, wd_ref, o_ref, h_ref, acc_ref, *maybe_tr_ref,
                tm, is_ctx):
    tr_ref = None if is_ctx else maybe_tr_ref[0]
    f = pl.program_id(1)
    row = _segment_mod_row(pl.program_id(0), tm, is_ctx)

    @pl.when(f == 0)
    def _():
        h = _norm_modulate(x_ref[...], nw_ref[...], sh_ref[pl.ds(row, 1), :], sc_ref[pl.ds(row, 1), :])
        h_ref[...] = h.astype(BF16)
        acc_ref[...] = jnp.zeros_like(acc_ref)

    u = jnp.dot(h_ref[...], wu_ref[...], preferred_element_type=F32)
    a = jnp.square(jnp.maximum(u, 0.0))
    acc_ref[...] += jnp.dot(a.astype(BF16), wd_ref[...], preferred_element_type=F32)

    @pl.when(f == pl.num_programs(1) - 1)
    def _():
        res = x_ref[...] + g_ref[pl.ds(row, 1), :] * acc_ref[...]
        if is_ctx:
            o_ref[...] = res
        else:
            lane_tiles = D_MODEL // LANES
            for c in range(lane_tiles):
                tr_ref[c] = res[:, c * LANES:(c + 1) * LANES]
            for b in range(GRID_W):
                for c in range(lane_tiles):
                    o_ref[b, :, c * LANES:(c + 1) * LANES] = tr_ref.at[c][pl.ds(b, tm // GRID_W, stride=GRID_W), :]


def mlp(r_seg, norm_w, mod_i, wu_bf16, wd_bf16, is_ctx):
    tm, tf = MLP_ROW_TILE, 1024
    rows = r_seg.shape[0]
    if is_ctx:
        out_spec = pl.BlockSpec((tm, D_MODEL), lambda t, f: (t, 0))
        out_shape = jax.ShapeDtypeStruct((rows, D_MODEL), F32)
    else:
        assert SEQ == GRID_W * GRID_W and tm % GRID_W == 0
        groups = tm // GRID_W
        per_batch = GRID_W // groups
        out_spec = pl.BlockSpec((GRID_W, groups, D_MODEL), lambda t, f: (t // per_batch, t % per_batch, 0))
        out_shape = jax.ShapeDtypeStruct((rows // GRID_W, GRID_W, D_MODEL), F32)
    out = pl.pallas_call(
        functools.partial(_mlp_kernel, tm=tm, is_ctx=is_ctx),
        grid=(rows // tm, D_FF // tf),
        in_specs=[pl.BlockSpec((tm, D_MODEL), lambda t, f: (t, 0)),
                  pl.BlockSpec((1, D_MODEL), lambda t, f: (0, 0)),
                  pl.BlockSpec((MOD_ROWS, D_MODEL), lambda t, f: (0, 3)),
                  pl.BlockSpec((MOD_ROWS, D_MODEL), lambda t, f: (0, 4)),
                  pl.BlockSpec((MOD_ROWS, D_MODEL), lambda t, f: (0, 5)),
                  pl.BlockSpec((D_MODEL, tf), lambda t, f: (0, f)),
                  pl.BlockSpec((tf, D_MODEL), lambda t, f: (f, 0))],
        out_specs=out_spec,
        out_shape=out_shape,
        scratch_shapes=[pltpu.VMEM((tm, D_MODEL), BF16), pltpu.VMEM((tm, D_MODEL), F32)] + (
            [] if is_ctx else [pltpu.VMEM((D_MODEL // LANES, tm, LANES), F32)]),
        compiler_params=_params("arbitrary", "arbitrary"),
        name="mlp_ctx" if is_ctx else "mlp",
    )(r_seg, norm_w.reshape(1, D_MODEL), mod_i, mod_i, mod_i, wu_bf16, wd_bf16)
    return out.reshape(rows, D_MODEL)


def _final_norm_kernel(x_ref, w_ref, o_ref):
    x = x_ref[...]
    o_ref[...] = x * lax.rsqrt(jnp.mean(x * x, axis=-1, keepdims=True) + EPS) * w_ref[...]


def final_norm(r, w):
    tm = 512
    return pl.pallas_call(
        _final_norm_kernel,
        grid=(N_LAT // tm,),
        in_specs=[pl.BlockSpec((tm, D_MODEL), lambda t: (t, 0)),
                  pl.BlockSpec((1, D_MODEL), lambda t: (0, 0))],
        out_specs=pl.BlockSpec((tm, D_MODEL), lambda t: (t, 0)),
        out_shape=jax.ShapeDtypeStruct((N_LAT, D_MODEL), F32),
        compiler_params=_params("arbitrary"),
        name="final_norm",
    )(r, w.reshape(1, D_MODEL))


LAT_TILES = SEQ // SEQ_TILE
N_SEQ_TILES = T_ROWS // SEQ_TILE
HALO_BLOCKS = SEQ_TILE // SUBLANES


def _seg_first(t):
    return jnp.logical_or(t >= BATCH * LAT_TILES, t % LAT_TILES == 0)


def _seg_last(t):
    return jnp.logical_or(t >= BATCH * LAT_TILES, t % LAT_TILES == LAT_TILES - 1)


def _conv4(ext_ref, cw_ref, tm):
    acc = cw_ref[0:1, :] * ext_ref[pl.ds(SUBLANES - 2, tm), :]
    acc = acc + cw_ref[1:2, :] * ext_ref[pl.ds(SUBLANES - 1, tm), :]
    acc = acc + cw_ref[2:3, :] * ext_ref[pl.ds(SUBLANES, tm), :]
    acc = acc + cw_ref[3:4, :] * ext_ref[pl.ds(SUBLANES + 1, tm), :]
    return acc


def _fill_halo(ext_ref, prev_ref, cur_ref, next_ref, t, tm):
    zero = jnp.zeros((SUBLANES, ext_ref.shape[1]), F32)
    ext_ref[pl.ds(0, SUBLANES), :] = jnp.where(_seg_first(t), zero, prev_ref[...])
    ext_ref[pl.ds(SUBLANES, tm), :] = cur_ref[...]
    ext_ref[pl.ds(SUBLANES + tm, SUBLANES), :] = jnp.where(_seg_last(t), zero, next_ref[...])


def _halo_specs(width, col_of):
    last_blk = T_ROWS // SUBLANES - 1
    return [pl.BlockSpec((SUBLANES, width), lambda t, *g: (jnp.maximum(t * HALO_BLOCKS - 1, 0), col_of(t, *g))),
            pl.BlockSpec((SEQ_TILE, width), lambda t, *g: (t, col_of(t, *g))),
            pl.BlockSpec((SUBLANES, width),
                         lambda t, *g: (jnp.minimum((t + 1) * HALO_BLOCKS, last_blk), col_of(t, *g)))]


def _lru_prep_kernel(py_ref, xp_ref, xc_ref, xn_ref, cw_ref, cb_ref, wg_ref, bg_ref, lam_ref,
                     y_ref, a_ref, u_ref, ext_ref):
    t = pl.program_id(0)
    tm = SEQ_TILE
    y_ref[...] = jax.nn.gelu(py_ref[...])
    _fill_halo(ext_ref, xp_ref, xc_ref, xn_ref, t, tm)
    xr = _conv4(ext_ref, cw_ref, tm) + cb_ref[...]
    xb = xr.astype(BF16)
    for d in range(2):
        gts = []
        for g in range(2):
            cols = []
            for n in range(LRU_BLOCKS):
                cols.append(jnp.dot(xb[:, n * LRU_BW:(n + 1) * LRU_BW], wg_ref[d, g, n],
                                    preferred_element_type=F32))
            gts.append(jnp.concatenate(cols, axis=1) + bg_ref[d, g])
        log_a = -LRU_C * jax.nn.sigmoid(gts[0]) * jax.nn.softplus(-lam_ref[d])
        a = jnp.exp(log_a)
        a_ref[d] = a
        one_minus_a2 = -jnp.tanh(log_a) * (a * a + 1.0)
        u_ref[d] = jnp.sqrt(one_minus_a2) * jax.nn.sigmoid(gts[1]) * xr


def lru_prep(p, conv_w, conv_b, w_gate_bf16, b_gate, lam):
    w = LRU_W
    full = lambda shape: pl.BlockSpec(shape, lambda t: (0,) * len(shape))
    return pl.pallas_call(
        _lru_prep_kernel,
        grid=(N_SEQ_TILES,),
        in_specs=[pl.BlockSpec((SEQ_TILE, w), lambda t: (t, 0))] + _halo_specs(w, lambda t: 1) + [
            full((CONV_W, w)), full((1, w)),
            full((2, 2, LRU_BLOCKS, LRU_BW, LRU_BW)), full((2, 2, 1, w)), full((2, 1, w))],
        out_specs=[pl.BlockSpec((SEQ_TILE, w), lambda t: (t, 0)),
                   pl.BlockSpec((2, SEQ_TILE, w), lambda t: (0, t, 0)),
                   pl.BlockSpec((2, SEQ_TILE, w), lambda t: (0, t, 0))],
        out_shape=[jax.ShapeDtypeStruct((T_ROWS, w), F32),
                   jax.ShapeDtypeStruct((2, T_ROWS, w), F32),
                   jax.ShapeDtypeStruct((2, T_ROWS, w), F32)],
        scratch_shapes=[pltpu.VMEM((SEQ_TILE + 2 * SUBLANES, w), F32)],
        compiler_params=_params("arbitrary"),
        name="lru_prep",
    )(p, p, p, p, conv_w, conv_b.reshape(1, w), w_gate_bf16, b_gate.reshape(2, 2, 1, w), lam.reshape(2, 1, w))


def _lru_scan_kernel(a_ref, u_ref, o_ref, carry_ref, *, reverse):
    @pl.when(pl.program_id(2) == 0)
    def _():
        carry_ref[...] = jnp.zeros_like(carry_ref)

    nblk = SEQ_TILE // SUBLANES
    width = a_ref.shape[-1]
    row = lax.broadcasted_iota(jnp.int32, (SUBLANES, width), 0)

    def body(i, carry):
        blk = (nblk - 1 - i) if reverse else i
        r0 = pl.multiple_of(blk * SUBLANES, SUBLANES)
        a = a_ref[0, pl.ds(r0, SUBLANES), :]
        u = u_ref[0, pl.ds(r0, SUBLANES), :]
        for s in (1, 2, 4):
            if reverse:
                keep = row < SUBLANES - s
                shift = SUBLANES - s
            else:
                keep = row >= s
                shift = s
            a_sh = jnp.where(keep, pltpu.roll(a, shift, 0), 1.0)
            u_sh = jnp.where(keep, pltpu.roll(u, shift, 0), 0.0)
            u = a * u_sh + u
            a = a * a_sh
        h = u + a * carry
        o_ref[pl.ds(r0, SUBLANES), :] = h
        return h[0:1, :] if reverse else h[SUBLANES - 1:SUBLANES, :]

    carry_ref[...] = lax.fori_loop(0, nblk, body, carry_ref[...])


def lru_scan(a, u, direction):
    reverse = direction == 1
    tw = 512

    def row_tile(b, k):
        lat = b * LAT_TILES + ((LAT_TILES - k) if reverse else (k - 1))
        return jnp.where(k == 0, BATCH * LAT_TILES + b, lat)

    spec3 = pl.BlockSpec((1, SEQ_TILE, tw), lambda b, l, k: (direction, row_tile(b, k), l))
    return pl.pallas_call(
        functools.partial(_lru_scan_kernel, reverse=reverse),
        grid=(BATCH, LRU_W // tw, LAT_TILES + 1),
        in_specs=[spec3, spec3],
        out_specs=pl.BlockSpec((SEQ_TILE, tw), lambda b, l, k: (row_tile(b, k), l)),
        out_shape=jax.ShapeDtypeStruct((T_ROWS, LRU_W), F32),
        scratch_shapes=[pltpu.VMEM((1, tw), F32)],
        compiler_params=_params("arbitrary", "arbitrary", "arbitrary"),
        name="lru_scan_bwd" if reverse else "lru_scan_fwd",
    )(a, u)


def _lru_post_kernel(hf_ref, hb_ref, y_ref, o_ref):
    o_ref[...] = ((hf_ref[...] + hb_ref[...]) * y_ref[...]).astype(BF16)


def lru_post(hf, hb, y):
    tm = 512
    spec = pl.BlockSpec((tm, LRU_W), lambda t: (t, 0))
    return pl.pallas_call(
        _lru_post_kernel,
        grid=(T_ROWS // tm,),
        in_specs=[spec, spec, spec],
        out_specs=spec,
        out_shape=jax.ShapeDtypeStruct((T_ROWS, LRU_W), BF16),
        compiler_params=_params("arbitrary"),
        name="lru_post",
    )(hf, hb, y)


def lru_mixer(p, conv_w, conv_b, w_gate, b_gate, lam):
    y, a, u = lru_prep(p, conv_w, conv_b, w_gate.astype(BF16), b_gate, lam)
    return lru_post(lru_scan(a, u, 0), lru_scan(a, u, 1), y)


PAIR_SLOTS = 16
NEG_BIG = -1e30


def _split3(x):
    hi = x.astype(BF16)
    r1 = x - hi.astype(F32)
    mid = r1.astype(BF16)
    lo = (r1 - mid.astype(F32)).astype(BF16)
    return hi, mid, lo


def _gates_kernel(x_ref, pa_ref, pb_ref, o_ref, orow_ref, *, kind, npairs):
    x = x_ref[...]
    tm = x.shape[0]
    lane = lax.broadcasted_iota(jnp.int32, x.shape, 1)
    if kind == "gdn":
        first = jax.nn.sigmoid(x)
        second = -jnp.exp(pa_ref[...]) * jax.nn.softplus(x + pb_ref[...])
    else:
        gt = GATE_CAP * jnp.tanh((x + pb_ref[...]) / GATE_CAP)
        first = gt
        second = jax.nn.log_sigmoid(gt)
    r = lax.broadcasted_iota(jnp.int32, (tm, tm), 0)
    s = lax.broadcasted_iota(jnp.int32, (tm, tm), 1)
    same = (r // CHUNK) == (s // CHUNK)
    as_w = lambda m: jnp.where(m, 1.0, 0.0).astype(BF16)
    parts = _split3(second)

    def chunk_sum(w):
        acc = jnp.dot(w, parts[0], preferred_element_type=F32)
        acc = acc + jnp.dot(w, parts[1], preferred_element_type=F32)
        return acc + jnp.dot(w, parts[2], preferred_element_type=F32)

    prefix = chunk_sum(as_w(jnp.logical_and(same, s <= r)))
    suffix = chunk_sum(as_w(jnp.logical_and(same, s >= r)))
    total = chunk_sum(as_w(same))
    cs = jnp.where((lane & 3) >= 2, suffix, prefix)
    x1 = jnp.where((lane & 7) < 4, first, cs)
    row_q = cs if kind == "gdn" else pltpu.roll(first, 4, 1) - cs
    row_t = row_q.T
    for j in range(npairs):
        o_ref[j] = jnp.concatenate([x1[:, 8 * j:8 * j + 8], total[:, 8 * j:8 * j + 8]], axis=1)
        orow_ref[j] = row_t[8 * j + 4:8 * j + 8, :]


def gate_columns(p, col_block, row_a, row_b, kind, npairs):
    return pl.pallas_call(
        functools.partial(_gates_kernel, kind=kind, npairs=npairs),
        grid=(N_SEQ_TILES,),
        in_specs=[pl.BlockSpec((SEQ_TILE, LANES), lambda t: (t, col_block)),
                  pl.BlockSpec((1, LANES), lambda t: (0, 0)),
                  pl.BlockSpec((1, LANES), lambda t: (0, 0))],
        out_specs=[pl.BlockSpec((npairs, SEQ_TILE, PAIR_SLOTS), lambda t: (0, t, 0)),
                   pl.BlockSpec((npairs, 4, SEQ_TILE), lambda t: (0, 0, t))],
        out_shape=[jax.ShapeDtypeStruct((npairs, T_ROWS, PAIR_SLOTS), F32),
                   jax.ShapeDtypeStruct((npairs, 4, T_ROWS), F32)],
        compiler_params=_params("arbitrary"),
        name=kind + "_gates",
    )(p, row_a, row_b)


STACK = 4 * CHUNK
CHUNKS_PER_TILE = SEQ_TILE // CHUNK


def _mm(a, b):
    return jnp.dot(a.astype(BF16), b.astype(BF16), preferred_element_type=F32)


def _mm_nt(a, b):
    return lax.dot_general(a.astype(BF16), b.astype(BF16), (((1,), (1,)), ((), ())),
                           preferred_element_type=F32)


def _stack_masks():
    r = lax.broadcasted_iota(jnp.int32, (STACK, STACK), 0)
    s = lax.broadcasted_iota(jnp.int32, (STACK, STACK), 1)
    same = (r // CHUNK) == (s // CHUNK)
    fwd = r < 2 * CHUNK
    ordered = jnp.logical_or(jnp.logical_and(fwd, s <= r), jnp.logical_and(jnp.logical_not(fwd), s >= r))
    incl = jnp.logical_and(same, ordered)
    strict = jnp.logical_and(incl, r != s)
    return r, s, incl, strict


def _col4(gf, gb, slot):
    return jnp.concatenate([gf[:, slot:slot + 1], gf[:, slot + 1:slot + 2],
                            gb[:, slot + 2:slot + 3], gb[:, slot + 3:slot + 4]], axis=0)


def _diag_blocks(x, rows0, width):
    return jnp.concatenate([x[rows0 + c * CHUNK:rows0 + (c + 1) * CHUNK, c * width:(c + 1) * width]
                            for c in range(4)], axis=0)


def _block_diag_place(v, r_col):
    chain = r_col // CHUNK
    return jnp.concatenate([jnp.where(chain == c, v, 0.0) for c in range(4)], axis=1)


def _chain_scale(col, width):
    blocks = []
    for c in range(4):
        t = jnp.broadcast_to(col[c * CHUNK:(c + 1) * CHUNK], (CHUNK, width))
        blocks.append(jnp.concatenate([t, t], axis=0))
    return jnp.concatenate(blocks, axis=1)


def _state_products(x, y, state, width):
    xs, ys = [], []
    for p in range(2):
        rows = slice(2 * p * CHUNK, (2 * p + 2) * CHUNK)
        out = _mm(jnp.concatenate([x[rows], y[rows]], axis=0), state[:, 2 * p * width:(2 * p + 2) * width])
        for e in range(2):
            cols = slice(e * width, (e + 1) * width)
            xs.append(out[e * CHUNK:(e + 1) * CHUNK, cols])
            ys.append(out[(2 + e) * CHUNK:(3 + e) * CHUNK, cols])
    return jnp.concatenate(xs, axis=0), jnp.concatenate(ys, axis=0)


def _scan_tile(b, k, reverse):
    lat = b * LAT_TILES + ((LAT_TILES - k) if reverse else (k - 1))
    return jnp.where(k == 0, BATCH * LAT_TILES + b, lat)


def _gdn_conv_kernel(xp_ref, xc_ref, xn_ref, cw_ref, o_ref, ext_ref):
    t = pl.program_id(0)
    ct = pl.program_id(1)
    _fill_halo(ext_ref, xp_ref, xc_ref, xn_ref, t, SEQ_TILE)
    y = _conv4(ext_ref, cw_ref, SEQ_TILE)
    y = y * jax.nn.sigmoid(y)
    qk_tiles = GDN_QK // GDN_CONV_TILE

    @pl.when(ct >= 2 * qk_tiles)
    def _():
        o_ref[...] = y

    @pl.when(ct < 2 * qk_tiles)
    def _():
        scale = jnp.where(ct < qk_tiles, GDN_DK ** -0.5, 1.0)
        for h in range(GDN_CONV_TILE // GDN_DK):
            blk = y[:, h * GDN_DK:(h + 1) * GDN_DK]
            nrm = blk * lax.rsqrt(jnp.sum(blk * blk, axis=-1, keepdims=True) + EPS)
            o_ref[:, h * GDN_DK:(h + 1) * GDN_DK] = nrm * scale


GDN_CONV_TILE = 1024


def gdn_conv(p, conv_w):
    n = 2 * GDN_QK + GDN_VW
    tn = GDN_CONV_TILE
    return pl.pallas_call(
        _gdn_conv_kernel,
        grid=(N_SEQ_TILES, n // tn),
        in_specs=_halo_specs(tn, lambda t, c: c) + [pl.BlockSpec((CONV_W, tn), lambda t, c: (0, c))],
        out_specs=pl.BlockSpec((SEQ_TILE, tn), lambda t, c: (t, c)),
        out_shape=jax.ShapeDtypeStruct((T_ROWS, n), F32),
        scratch_shapes=[pltpu.VMEM((SEQ_TILE + 2 * SUBLANES, tn), F32)],
        compiler_params=_params("arbitrary", "arbitrary"),
        name="gdn_conv",
    )(p, p, p, conv_w)


def _row4(rf_ref, rb_ref, hp, rf, rb):
    return jnp.concatenate([rf_ref[hp, 0:1, rf:rf + CHUNK], rf_ref[hp, 1:2, rf:rf + CHUNK],
                            rb_ref[hp, 2:3, rb:rb + CHUNK], rb_ref[hp, 3:4, rb:rb + CHUNK]], axis=1)


GDN_STEP_PAIRS = 2


def _gdn_core_kernel(qf_ref, kf_ref, vf_ref, gf_ref, grf_ref, qb_ref, kb_ref, vb_ref, gb_ref, grb_ref,
                     of_ref, ob_ref, s_ref):
    @pl.when(pl.program_id(2) == 0)
    def _():
        s_ref[...] = jnp.zeros_like(s_ref)

    r, s, incl, strict = _stack_masks()
    eye = r == s
    blk16 = (r // 16) == (s // 16)
    blk32 = (r // 32) == (s // 32)
    r_col = lax.broadcasted_iota(jnp.int32, (STACK, 1), 0)

    pairs = range(GDN_STEP_PAIRS)
    steps = range(CHUNKS_PER_TILE)
    rows_f = [n * CHUNK for n in steps]
    rows_b = [(CHUNKS_PER_TILE - 1 - n) * CHUNK for n in steps]
    problems = [(hp, n) for n in steps for hp in pairs]
    each = lambda fn, *lists: [fn(*xs) for xs in zip(*lists)]

    def load(hp, n):
        rf, rb = rows_f[n], rows_b[n]
        dk = slice(hp * GDN_DK, (hp + 1) * GDN_DK)
        v_of = lambda e: slice((2 * hp + e) * GDN_DV, (2 * hp + e + 1) * GDN_DV)
        kf = kf_ref[pl.ds(rf, CHUNK), dk]
        kb = kb_ref[pl.ds(rb, CHUNK), dk]
        qf = qf_ref[pl.ds(rf, CHUNK), dk]
        qb = qb_ref[pl.ds(rb, CHUNK), dk]
        k4 = jnp.concatenate([kf, kf, kb, kb], axis=0)
        q4 = jnp.concatenate([qf, qf, qb, qb], axis=0)
        v4 = jnp.concatenate([vf_ref[pl.ds(rf, CHUNK), v_of(0)], vf_ref[pl.ds(rf, CHUNK), v_of(1)],
                              vb_ref[pl.ds(rb, CHUNK), v_of(0)], vb_ref[pl.ds(rb, CHUNK), v_of(1)]], axis=0)
        gf = gf_ref[hp, pl.ds(rf, CHUNK), :]
        gb = gb_ref[hp, pl.ds(rb, CHUNK), :]
        return (k4, q4, v4, _col4(gf, gb, 0), _col4(gf, gb, 4), _col4(gf, gb, 12),
                _row4(grf_ref, grb_ref, hp, rf, rb))

    k4, q4, v4, beta, gc, gl, gc_row = zip(*[load(hp, n) for hp, n in problems])
    eg = each(jnp.exp, gc)
    gram = each(lambda k, q: _mm_nt(jnp.concatenate([k, q], axis=0), k), k4, q4)
    decay = each(lambda g, g_row: jnp.exp(jnp.where(incl, g - g_row, NEG_BIG)), gc, gc_row)
    nmat = each(lambda gm, d, b: jnp.where(strict, gm[:STACK] * d * b, 0.0), gram, decay, beta)
    a_qk = each(lambda gm, d: gm[STACK:] * d, gram, decay)

    n0 = each(lambda x: jnp.where(blk16, x, 0.0), nmat)
    p2 = each(_mm, n0, n0)
    p4 = each(_mm, p2, p2)
    t = each(lambda x: jnp.where(eye, 1.0, 0.0) - x, n0)
    t = each(lambda a, b: a + _mm(a, b), t, p2)
    p8 = each(_mm, p4, p4)
    t = each(lambda a, b: a + _mm(a, b), t, p4)
    t = each(lambda a, b: a + _mm(a, b), t, p8)
    for level_mask in (jnp.logical_and(blk32, jnp.logical_not(blk16)), jnp.logical_not(blk32)):
        tn = each(lambda a, x: _mm(a, jnp.where(level_mask, x, 0.0)), t, nmat)
        t = each(lambda a, b: a - _mm(b, a), t, tn)
    uw = each(lambda a, v, k, b, e: _mm(a, jnp.concatenate([v * b, k * (b * e)], axis=1)),
              t, v4, k4, beta, eg)
    qe = each(lambda q, e: q * e, q4, eg)
    k_t = each(lambda k, g, gt: (k * jnp.exp(gt - g)).T, k4, gc, gl)
    s_scale = each(lambda gt: _chain_scale(jnp.exp(gt), GDN_DV), gl)

    s_all = [s_ref[hp] for hp in pairs]
    for n in steps:
        for hp in pairs:
            i = n * GDN_STEP_PAIRS + hp
            ws, qs = _state_products(uw[i][:, GDN_DV:], qe[i], s_all[hp], GDN_DV)
            v_new = uw[i][:, :GDN_DV] - ws
            o4 = qs + _mm(a_qk[i], v_new)
            s_all[hp] = s_all[hp] * s_scale[i] + _mm(k_t[i], _block_diag_place(v_new, r_col))
            rf, rb = rows_f[n], rows_b[n]
            for c, (o_ref, r0) in enumerate(((of_ref, rf), (of_ref, rf), (ob_ref, rb), (ob_ref, rb))):
                col = (2 * hp + c % 2) * GDN_DV
                o_ref[pl.ds(r0, CHUNK), col:col + GDN_DV] = o4[c * CHUNK:(c + 1) * CHUNK]
    for hp in pairs:
        s_ref[hp] = s_all[hp]


def gdn_core(qkv, gcols, grows):
    hp = GDN_STEP_PAIRS
    kcol = GDN_QK // (hp * GDN_DK)
    vcol = 2 * GDN_QK // (hp * 2 * GDN_DV)

    def specs(reverse):
        tile = lambda b, j, k: _scan_tile(b, k, reverse)
        return [pl.BlockSpec((SEQ_TILE, hp * GDN_DK), lambda b, j, k: (tile(b, j, k), j)),
                pl.BlockSpec((SEQ_TILE, hp * GDN_DK), lambda b, j, k: (tile(b, j, k), kcol + j)),
                pl.BlockSpec((SEQ_TILE, hp * 2 * GDN_DV), lambda b, j, k: (tile(b, j, k), vcol + j)),
                pl.BlockSpec((hp, SEQ_TILE, PAIR_SLOTS), lambda b, j, k: (j, tile(b, j, k), 0)),
                pl.BlockSpec((hp, 4, SEQ_TILE), lambda b, j, k: (j, 0, tile(b, j, k)))]

    out_spec = lambda reverse: pl.BlockSpec((SEQ_TILE, hp * 2 * GDN_DV),
                                            lambda b, j, k: (_scan_tile(b, k, reverse), j))
    return pl.pallas_call(
        _gdn_core_kernel,
        grid=(BATCH, GDN_QK_HEADS // hp, LAT_TILES + 1),
        in_specs=specs(False) + specs(True),
        out_specs=[out_spec(False), out_spec(True)],
        out_shape=[jax.ShapeDtypeStruct((T_ROWS, GDN_VW), F32)] * 2,
        scratch_shapes=[pltpu.VMEM((hp, GDN_DK, 4 * GDN_DV), F32)],
        compiler_params=_params("arbitrary", "arbitrary", "arbitrary"),
        name="gdn_core",
    )(qkv, qkv, qkv, gcols, grows, qkv, qkv, qkv, gcols, grows)


def _head_norm_gate_kernel(of_ref, ob_ref, z_ref, nw_ref, y_ref, *, head_dim, gate):
    for h in range(of_ref.shape[1] // head_dim):
        sl = slice(h * head_dim, (h + 1) * head_dim)
        o = of_ref[:, sl] + ob_ref[:, sl]
        y = o * lax.rsqrt(jnp.mean(o * o, axis=-1, keepdims=True) + EPS) * nw_ref[:, sl]
        z = z_ref[:, sl]
        g = z * jax.nn.sigmoid(z) if gate == "silu" else jax.nn.sigmoid(z)
        y_ref[:, sl] = (y * g).astype(BF16)


def head_norm_gate(o_f, o_b, p, z_col_block, norm_row, head_dim, gate):
    width = o_f.shape[1]
    tm, tn = 512, 512
    spec = pl.BlockSpec((tm, tn), lambda t, c: (t, c))
    return pl.pallas_call(
        functools.partial(_head_norm_gate_kernel, head_dim=head_dim, gate=gate),
        grid=(T_ROWS // tm, width // tn),
        in_specs=[spec, spec, pl.BlockSpec((tm, tn), lambda t, c: (t, z_col_block + c)),
                  pl.BlockSpec((1, tn), lambda t, c: (0, c))],
        out_specs=spec,
        out_shape=jax.ShapeDtypeStruct((T_ROWS, width), BF16),
        compiler_params=_params("arbitrary", "arbitrary"),
        name="head_norm_" + gate,
    )(o_f, o_b, p, norm_row)


def gdn_mixer(p, conv_w, a_log, dt_bias, norm_w):
    zeros4 = jnp.zeros((GDN_QK_HEADS, 4), F32)
    per_pair = lambda t: t.reshape(2, GDN_QK_HEADS, 2).transpose(1, 0, 2).reshape(GDN_QK_HEADS, 4)
    row = lambda t: jnp.concatenate([zeros4, per_pair(t)], axis=1).reshape(1, LANES)
    gcols, grows = gate_columns(p, (2 * GDN_QK + 2 * GDN_VW) // LANES, row(a_log), row(dt_bias), "gdn",
                                GDN_QK_HEADS)
    qkv = gdn_conv(p, conv_w)
    o_f, o_b = gdn_core(qkv, gcols, grows)
    norm_row = jnp.tile(norm_w, GDN_V_HEADS).reshape(1, GDN_VW)
    return head_norm_gate(o_f, o_b, p, (2 * GDN_QK + GDN_VW) // 512, norm_row, GDN_DV, "silu")


def gdn_in_weight(w_in):
    base = 2 * GDN_QK + 2 * GDN_VW
    gates = w_in[:, base:].reshape(D_MODEL, 2, 2, GDN_QK_HEADS, 2)
    gates = gates.transpose(0, 3, 1, 2, 4).reshape(D_MODEL, 4 * GDN_V_HEADS)
    return _pad_cols(jnp.concatenate([w_in[:, :base], gates], axis=1), COL_TILE)


ML_PAIRS = ML_HEADS // 2


def _mlstm_core_kernel(qf_ref, kf_ref, vf_ref, gf_ref, grf_ref, qb_ref, kb_ref, vb_ref, gb_ref, grb_ref,
                       of_ref, ob_ref, c_ref, n_ref, m_ref):
    @pl.when(pl.program_id(2) == 0)
    def _():
        c_ref[...] = jnp.zeros_like(c_ref)
        n_ref[...] = jnp.zeros_like(n_ref)
        m_ref[...] = jnp.zeros_like(m_ref)

    r, s, incl, _ = _stack_masks()
    r_col = lax.broadcasted_iota(jnp.int32, (STACK, 1), 0)
    last_rows = (CHUNK - 1, 2 * CHUNK - 1, 2 * CHUNK, 3 * CHUNK)

    steps = range(CHUNKS_PER_TILE)
    rows_f = [n * CHUNK for n in steps]
    rows_b = [(CHUNKS_PER_TILE - 1 - n) * CHUNK for n in steps]
    each = lambda fn, *lists: [fn(*xs) for xs in zip(*lists)]

    def load(rf, rb):
        def stack(f_ref, b_ref, w):
            return jnp.concatenate([f_ref[pl.ds(rf, CHUNK), 0:w], f_ref[pl.ds(rf, CHUNK), w:2 * w],
                                    b_ref[pl.ds(rb, CHUNK), 0:w], b_ref[pl.ds(rb, CHUNK), w:2 * w]], axis=0)

        gf = gf_ref[0, pl.ds(rf, CHUNK), :]
        gb = gb_ref[0, pl.ds(rb, CHUNK), :]
        return (stack(qf_ref, qb_ref, ML_DQK) * ML_DQK ** -0.5, stack(kf_ref, kb_ref, ML_DQK),
                stack(vf_ref, vb_ref, ML_DV), _col4(gf, gb, 0), _col4(gf, gb, 4), _col4(gf, gb, 12),
                _row4(grf_ref, grb_ref, 0, rf, rb))

    q4, k4, v4, ig, bc, bl, ib_row = zip(*each(load, rows_f, rows_b))
    d_log = each(lambda b, row: jnp.where(incl, b + row, NEG_BIG), bc, ib_row)
    row_max = each(lambda d: jnp.max(d, axis=1, keepdims=True), d_log)
    w_max = each(lambda rm: jnp.concatenate([jnp.broadcast_to(rm[i:i + 1], (CHUNK, 1)) for i in last_rows], axis=0),
                 row_max)
    qk = each(_mm_nt, q4, k4)

    m_in = [m_ref[...]]
    for n in steps:
        m_in.append(jnp.maximum(bl[n] + m_in[n], w_max[n]))
    m_out = m_in[1:]
    m_ref[...] = m_in[-1]

    inter = each(lambda b, m: b + m, bc, m_in)
    m_t = each(jnp.maximum, inter, row_max)
    smat = each(lambda x, d, mt: x * jnp.exp(d - mt), qk, d_log, m_t)
    w_inter = each(lambda i, mt: jnp.exp(i - mt), inter, m_t)
    intra = each(_mm, smat, v4)
    s_sum = each(lambda x: jnp.sum(x, axis=1, keepdims=True), smat)
    cd = each(lambda b, m0, m1: jnp.exp(b + m0 - m1), bl, m_in, m_out)
    kw = each(lambda k, b, c, i, m1: k * jnp.exp(b - c + i - m1), k4, bl, bc, ig, m_out)
    d_c = each(lambda x, v: _mm(x.T, _block_diag_place(v, r_col)), kw, v4)
    c_scale = each(lambda x: _chain_scale(x, ML_DV), cd)

    c_all = c_ref[...]
    n_all = [n_ref[c:c + 1, :] for c in range(4)]
    for n in steps:
        qc = []
        for p in range(2):
            out = _mm(q4[n][2 * p * CHUNK:(2 * p + 2) * CHUNK], c_all[:, 2 * p * ML_DV:(2 * p + 2) * ML_DV])
            qc += [out[e * CHUNK:(e + 1) * CHUNK, e * ML_DV:(e + 1) * ML_DV] for e in range(2)]
        qc = jnp.concatenate(qc, axis=0)
        num = w_inter[n] * qc + intra[n]
        n_rows = jnp.concatenate([jnp.broadcast_to(x, (CHUNK, ML_DQK)) for x in n_all], axis=0)
        den = w_inter[n] * jnp.sum(q4[n] * n_rows, axis=1, keepdims=True) + s_sum[n]
        h4 = num / jnp.maximum(jnp.abs(den), jnp.exp(-m_t[n]))
        c_all = c_all * c_scale[n] + d_c[n]
        n_all = [cd[n][c * CHUNK:c * CHUNK + 1] * n_all[c]
                 + jnp.sum(kw[n][c * CHUNK:(c + 1) * CHUNK], axis=0, keepdims=True) for c in range(4)]
        rf, rb = rows_f[n], rows_b[n]
        of_ref[pl.ds(rf, CHUNK), 0:ML_DV] = h4[0:CHUNK]
        of_ref[pl.ds(rf, CHUNK), ML_DV:2 * ML_DV] = h4[CHUNK:2 * CHUNK]
        ob_ref[pl.ds(rb, CHUNK), 0:ML_DV] = h4[2 * CHUNK:3 * CHUNK]
        ob_ref[pl.ds(rb, CHUNK), ML_DV:2 * ML_DV] = h4[3 * CHUNK:4 * CHUNK]
    c_ref[...] = c_all
    for c in range(4):
        n_ref[c:c + 1, :] = n_all[c]


def mlstm_core(p, gcols, grows):
    kcol = ML_QK // (2 * ML_DQK)
    vcol = 2 * ML_QK // (2 * ML_DV)

    def specs(reverse):
        tile = lambda b, j, k: _scan_tile(b, k, reverse)
        return [pl.BlockSpec((SEQ_TILE, 2 * ML_DQK), lambda b, j, k: (tile(b, j, k), j)),
                pl.BlockSpec((SEQ_TILE, 2 * ML_DQK), lambda b, j, k: (tile(b, j, k), kcol + j)),
                pl.BlockSpec((SEQ_TILE, 2 * ML_DV), lambda b, j, k: (tile(b, j, k), vcol + j)),
                pl.BlockSpec((1, SEQ_TILE, PAIR_SLOTS), lambda b, j, k: (j, tile(b, j, k), 0)),
                pl.BlockSpec((1, 4, SEQ_TILE), lambda b, j, k: (j, 0, tile(b, j, k)))]

    out_spec = lambda reverse: pl.BlockSpec((SEQ_TILE, 2 * ML_DV),
                                            lambda b, j, k: (_scan_tile(b, k, reverse), j))
    return pl.pallas_call(
        _mlstm_core_kernel,
        grid=(BATCH, ML_PAIRS, LAT_TILES + 1),
        in_specs=specs(False) + specs(True),
        out_specs=[out_spec(False), out_spec(True)],
        out_shape=[jax.ShapeDtypeStruct((T_ROWS, ML_V), F32)] * 2,
        scratch_shapes=[pltpu.VMEM((ML_DQK, 4 * ML_DV), F32), pltpu.VMEM((SUBLANES, ML_DQK), F32),
                        pltpu.VMEM((STACK, 1), F32)],
        compiler_params=_params("arbitrary", "arbitrary", "arbitrary"),
        name="mlstm_core",
    )(p, p, p, gcols, grows, p, p, p, gcols, grows)


def mlstm_mixer(p, gate_b, norm_w):
    bias = gate_b.reshape(2, 2, ML_PAIRS, 2).transpose(2, 1, 0, 3).reshape(1, 4 * ML_HEADS)
    bias = jnp.pad(bias, ((0, 0), (0, LANES - 4 * ML_HEADS)))
    gcols, grows = gate_columns(p, (2 * ML_QK + 2 * ML_V) // LANES, bias, bias, "mlstm", ML_PAIRS)
    h_f, h_b = mlstm_core(p, gcols, grows)
    return head_norm_gate(h_f, h_b, p, (2 * ML_QK + ML_V) // 512, norm_w.reshape(1, ML_V), ML_DV, "sigmoid")


def mlstm_in_weight(w_in):
    base = 2 * ML_QK + 2 * ML_V
    gates = w_in[:, base:].reshape(D_MODEL, 2, 2, ML_PAIRS, 2)
    gates = gates.transpose(0, 3, 2, 1, 4).reshape(D_MODEL, 4 * ML_HEADS)
    return _pad_cols(jnp.concatenate([w_in[:, :base], gates], axis=1), COL_TILE)


def _pad_cols(w, mult):
    n = w.shape[1]
    return jnp.pad(w, ((0, 0), (0, (-n) % mult)))


def kernel(x, c, ctx, c_ctx, w_mod, b_mod, norm_mix, norm_ff, norm_out, gdn_w_in, gdn_conv, gdn_a_log, gdn_dt_bias, gdn_norm, gdn_w_out, ml_w_in, ml_gate_b, ml_norm, ml_w_out, lru_w_in, lru_conv, lru_conv_b, lru_w_gate, lru_b_gate, lru_lambda, lru_w_out, ff_w_up, ff_w_down):
    assert DEPTH % 2 == 0
    r_lat = x.reshape(N_LAT, D_MODEL)
    r_ctx = ctx.reshape(N_CTX, D_MODEL)
    c_all = jnp.concatenate([c, c_ctx[None], jnp.zeros((MOD_ROWS - BATCH - 1, D_MODEL), F32)], axis=0)
    mod = modulation_table(c_all, w_mod, b_mod)
    for i in range(DEPTH):
        kind, j = i % N_MIXERS, i // N_MIXERS
        if kind == 0:
            p = in_proj(r_lat, r_ctx, norm_mix[i], mod[i], gdn_in_weight(gdn_w_in[j]).astype(BF16))
            y = gdn_mixer(p, gdn_conv[j], gdn_a_log[j], gdn_dt_bias[j], gdn_norm[j])
            w_out = gdn_w_out[j]
        elif kind == 1:
            p = in_proj(r_lat, r_ctx, norm_mix[i], mod[i], mlstm_in_weight(ml_w_in[j]).astype(BF16))
            y = mlstm_mixer(p, ml_gate_b[j], ml_norm[j])
            w_out = ml_w_out[j]
        else:
            p = in_proj(r_lat, r_ctx, norm_mix[i], mod[i], lru_w_in[j].astype(BF16))
            y = lru_mixer(p, lru_conv[j], lru_conv_b[j], lru_w_gate[j], lru_b_gate[j], lru_lambda[j])
            w_out = lru_w_out[j]
        w_out, w_up, w_down = w_out.astype(BF16), ff_w_up[i].astype(BF16), ff_w_down[i].astype(BF16)
        r_lat = out_proj(y, w_out, r_lat, mod[i], is_ctx=False)
        r_lat = mlp(r_lat, norm_ff[i], mod[i], w_up, w_down, is_ctx=False)
        if i < DEPTH - 1:
            r_ctx = out_proj(y, w_out, r_ctx, mod[i], is_ctx=True)
            r_ctx = mlp(r_ctx, norm_ff[i], mod[i], w_up, w_down, is_ctx=True)
    return final_norm(r_lat, norm_out).reshape(BATCH, SEQ, D_MODEL)
```

```python
import functools
import math

import jax
import jax.numpy as jnp
from jax import lax
from jax.experimental import pallas as pl
from jax.experimental.pallas import tpu as pltpu

F32 = jnp.float32
BF16 = jnp.bfloat16

D_MODEL = 2048
BATCH = 4
SEQ = 4096
DEPTH = 4
CTX_LEN = 256
GRID_W = 64
N_MIXERS = 3
CHUNK = 64
CONV_W = 4
EPS = 1e-6
D_FF = 4 * D_MODEL

GDN_QK_HEADS = D_MODEL // 128
GDN_V_HEADS = 2 * GDN_QK_HEADS
GDN_DK = 128
GDN_DV = 128
GDN_QK = GDN_QK_HEADS * GDN_DK
GDN_VW = GDN_V_HEADS * GDN_DV
GDN_REP = GDN_V_HEADS // GDN_QK_HEADS
GDN_IN = 2 * GDN_QK + 2 * GDN_VW + 4 * GDN_V_HEADS

ML_HEADS = 8
ML_DQK = D_MODEL // (2 * ML_HEADS)
ML_DV = D_MODEL // ML_HEADS
ML_QK = ML_HEADS * ML_DQK
ML_V = ML_HEADS * ML_DV
ML_IN = 2 * ML_QK + 2 * ML_V + 4 * ML_HEADS
GATE_CAP = 15.0

LRU_W = D_MODEL
LRU_BLOCKS = 8
LRU_BW = LRU_W // LRU_BLOCKS
LRU_C = 8.0

N_LAT = BATCH * SEQ
N_CTX = BATCH * CTX_LEN
T_ROWS = N_LAT + N_CTX
MOD_ROWS = 8
CTX_MOD_ROW = BATCH

V7X_VMEM_BYTES = 64 * 1024 * 1024
VMEM_LIMIT = 56 * 1024 * 1024
SUBLANES = 8
LANES = 128

ROW_TILE = 1024
MLP_ROW_TILE = 512
COL_TILE = 512
IN_PROJ_MAX_COL_TILE = 1280
SEQ_TILE = 256


def _params(*sem):
    return pltpu.CompilerParams(dimension_semantics=sem, vmem_limit_bytes=VMEM_LIMIT)


def _mod_row(tile, tile_rows):
    lat_tiles = N_LAT // tile_rows
    return jnp.where(tile < lat_tiles, tile // (SEQ // tile_rows), CTX_MOD_ROW)


def _mod_kernel(c_ref, w_ref, b_ref, o_ref):
    c = c_ref[...]
    s = c * jax.nn.sigmoid(c)
    o_ref[0] = jnp.dot(s.astype(BF16), w_ref[0].astype(BF16), preferred_element_type=F32) + b_ref[0]


def modulation_table(c_all, w_mod, b_mod):
    tn = 1024
    n6 = 6 * D_MODEL
    return pl.pallas_call(
        _mod_kernel,
        grid=(DEPTH, n6 // tn),
        in_specs=[pl.BlockSpec((MOD_ROWS, D_MODEL), lambda i, n: (0, 0)),
                  pl.BlockSpec((1, D_MODEL, tn), lambda i, n: (i, 0, n)),
                  pl.BlockSpec((1, 1, tn), lambda i, n: (i, 0, n))],
        out_specs=pl.BlockSpec((1, MOD_ROWS, tn), lambda i, n: (i, 0, n)),
        out_shape=jax.ShapeDtypeStruct((DEPTH, MOD_ROWS, n6), F32),
        compiler_params=_params("arbitrary", "arbitrary"),
        name="modulation_table",
    )(c_all, w_mod, b_mod.reshape(DEPTH, 1, n6))


def _norm_modulate(x, nw, shift, scale):
    y = x * lax.rsqrt(jnp.mean(x * x, axis=-1, keepdims=True) + EPS)
    return (y * nw) * (1.0 + scale) + shift


def _in_proj_kernel(xl_ref, xc_ref, nw_ref, sh_ref, sc_ref, w_ref, o_ref, h_ref, *, tm):
    t = pl.program_id(0)
    row = _mod_row(t, tm)
    lat_tiles = N_LAT // tm

    def prologue(x_ref):
        h = _norm_modulate(x_ref[...], nw_ref[...], sh_ref[pl.ds(row, 1), :], sc_ref[pl.ds(row, 1), :])
        h_ref[...] = h.astype(BF16)

    @pl.when(jnp.logical_and(pl.program_id(1) == 0, t < lat_tiles))
    def _():
        prologue(xl_ref)

    @pl.when(jnp.logical_and(pl.program_id(1) == 0, t >= lat_tiles))
    def _():
        prologue(xc_ref)

    o_ref[...] = jnp.dot(h_ref[...], w_ref[...], preferred_element_type=F32)


def in_proj(r_lat, r_ctx, norm_w, mod_i, w_bf16):
    n = w_bf16.shape[1]
    tm = ROW_TILE
    tn = max(d * LANES for d in range(1, IN_PROJ_MAX_COL_TILE // LANES + 1) if (n // LANES) % d == 0)
    lat_tiles = N_LAT // tm
    assert N_CTX == tm and n % LANES == 0
    once = pl.Buffered(1)
    return pl.pallas_call(
        functools.partial(_in_proj_kernel, tm=tm),
        grid=(T_ROWS // tm, n // tn),
        in_specs=[pl.BlockSpec((tm, D_MODEL), lambda t, j: (jnp.minimum(t, lat_tiles - 1), 0), pipeline_mode=once),
                  pl.BlockSpec((tm, D_MODEL), lambda t, j: (0, 0), pipeline_mode=once),
                  pl.BlockSpec((1, D_MODEL), lambda t, j: (0, 0)),
                  pl.BlockSpec((MOD_ROWS, D_MODEL), lambda t, j: (0, 0)),
                  pl.BlockSpec((MOD_ROWS, D_MODEL), lambda t, j: (0, 1)),
                  pl.BlockSpec((D_MODEL, tn), lambda t, j: (0, j))],
        out_specs=pl.BlockSpec((tm, tn), lambda t, j: (t, j)),
        out_shape=jax.ShapeDtypeStruct((T_ROWS, n), F32),
        scratch_shapes=[pltpu.VMEM((tm, D_MODEL), BF16)],
        compiler_params=_params("arbitrary", "arbitrary"),
        name="in_proj",
    )(r_lat, r_ctx, norm_w.reshape(1, D_MODEL), mod_i, mod_i, w_bf16)


def _segment_mod_row(tile, tile_rows, is_ctx):
    return CTX_MOD_ROW if is_ctx else tile // (SEQ // tile_rows)


def _out_proj_kernel(y_ref, w_ref, r_ref, g_ref, o_ref, *, tm, is_ctx):
    row = _segment_mod_row(pl.program_id(0), tm, is_ctx)
    acc = jnp.dot(y_ref[...], w_ref[...], preferred_element_type=F32)
    o_ref[...] = r_ref[...] + g_ref[pl.ds(row, 1), :] * acc


def out_proj(y_bf16, w_bf16, r_seg, mod_i, is_ctx):
    k = y_bf16.shape[1]
    tm, tn = ROW_TILE, COL_TILE
    nj = D_MODEL // tn
    rows = r_seg.shape[0]
    tile0 = N_LAT // tm if is_ctx else 0
    return pl.pallas_call(
        functools.partial(_out_proj_kernel, tm=tm, is_ctx=is_ctx),
        grid=(rows // tm, nj),
        in_specs=[pl.BlockSpec((tm, k), lambda t, j: (tile0 + t, 0)),
                  pl.BlockSpec((k, tn), lambda t, j: (0, j)),
                  pl.BlockSpec((tm, tn), lambda t, j: (t, j)),
                  pl.BlockSpec((MOD_ROWS, tn), lambda t, j: (0, 2 * nj + j))],
        out_specs=pl.BlockSpec((tm, tn), lambda t, j: (t, j)),
        out_shape=jax.ShapeDtypeStruct((rows, D_MODEL), F32),
        compiler_params=_params("arbitrary", "arbitrary"),
        name="out_proj_ctx" if is_ctx else "out_proj",
    )(y_bf16, w_bf16, r_seg, mod_i)


def _mlp_kernel(x_ref, nw_ref, sh_ref, sc_ref, g_ref, wu_ref, wd_ref, o_ref, h_ref, acc_ref, *maybe_tr_ref,
                tm, is_ctx):
    tr_ref = None if is_ctx else maybe_tr_ref[0]
    f = pl.program_id(1)
    row = _segment_mod_row(pl.program_id(0), tm, is_ctx)

    @pl.when(f == 0)
    def _():
        h = _norm_modulate(x_ref[...], nw_ref[...], sh_ref[pl.ds(row, 1), :], sc_ref[pl.ds(row, 1), :])
        h_ref[...] = h.astype(BF16)
        acc_ref[...] = jnp.zeros_like(acc_ref)

    u = jnp.dot(h_ref[...], wu_ref[...], preferred_element_type=F32)
    a = jnp.square(jnp.maximum(u, 0.0))
    acc_ref[...] += jnp.dot(a.astype(BF16), wd_ref[...], preferred_element_type=F32)

    @pl.when(f == pl.num_programs(1) - 1)
    def _():
        res = x_ref[...] + g_ref[pl.ds(row, 1), :] * acc_ref[...]
        if is_ctx:
            o_ref[...] = res
        else:
            lane_tiles = D_MODEL // LANES
            for c in range(lane_tiles):
                tr_ref[c] = res[:, c * LANES:(c + 1) * LANES]
            for b in range(GRID_W):
                for c in range(lane_tiles):
                    o_ref[b, :, c * LANES:(c + 1) * LANES] = tr_ref.at[c][pl.ds(b, tm // GRID_W, stride=GRID_W), :]


def mlp(r_seg, norm_w, mod_i, wu_bf16, wd_bf16, is_ctx):
    tm, tf = MLP_ROW_TILE, 1024
    rows = r_seg.shape[0]
    if is_ctx:
        out_spec = pl.BlockSpec((tm, D_MODEL), lambda t, f: (t, 0))
        out_shape = jax.ShapeDtypeStruct((rows, D_MODEL), F32)
    else:
        assert SEQ == GRID_W * GRID_W and tm % GRID_W == 0
        groups = tm // GRID_W
        per_batch = GRID_W // groups
        out_spec = pl.BlockSpec((GRID_W, groups, D_MODEL), lambda t, f: (t // per_batch, t % per_batch, 0))
        out_shape = jax.ShapeDtypeStruct((rows // GRID_W, GRID_W, D_MODEL), F32)
    out = pl.pallas_call(
        functools.partial(_mlp_kernel, tm=tm, is_ctx=is_ctx),
        grid=(rows // tm, D_FF // tf),
        in_specs=[pl.BlockSpec((tm, D_MODEL), lambda t, f: (t, 0)),
                  pl.BlockSpec((1, D_MODEL), lambda t, f: (0, 0)),
                  pl.BlockSpec((MOD_ROWS, D_MODEL), lambda t, f: (0, 3)),
                  pl.BlockSpec((MOD_ROWS, D_MODEL), lambda t, f: (0, 4)),
                  pl.BlockSpec((MOD_ROWS, D_MODEL), lambda t, f: (0, 5)),
                  pl.BlockSpec((D_MODEL, tf), lambda t, f: (0, f)),
                  pl.BlockSpec((tf, D_MODEL), lambda t, f: (f, 0))],
        out_specs=out_spec,
        out_shape=out_shape,
        scratch_shapes=[pltpu.VMEM((tm, D_MODEL), BF16), pltpu.VMEM((tm, D_MODEL), F32)] + (
            [] if is_ctx else [pltpu.VMEM((D_MODEL // LANES, tm, LANES), F32)]),
        compiler_params=_params("arbitrary", "arbitrary"),
        name="mlp_ctx" if is_ctx else "mlp",
    )(r_seg, norm_w.reshape(1, D_MODEL), mod_i, mod_i, mod_i, wu_bf16, wd_bf16)
    return out.reshape(rows, D_MODEL)


def _final_norm_kernel(x_ref, w_ref, o_ref):
    x = x_ref[...]
    o_ref[...] = x * lax.rsqrt(jnp.mean(x * x, axis=-1, keepdims=True) + EPS) * w_ref[...]


def final_norm(r, w):
    tm = 512
    return pl.pallas_call(
        _final_norm_kernel,
        grid=(N_LAT // tm,),
        in_specs=[pl.BlockSpec((tm, D_MODEL), lambda t: (t, 0)),
                  pl.BlockSpec((1, D_MODEL), lambda t: (0, 0))],
        out_specs=pl.BlockSpec((tm, D_MODEL), lambda t: (t, 0)),
        out_shape=jax.ShapeDtypeStruct((N_LAT, D_MODEL), F32),
        compiler_params=_params("arbitrary"),
        name="final_norm",
    )(r, w.reshape(1, D_MODEL))


LAT_TILES = SEQ // SEQ_TILE
N_SEQ_TILES = T_ROWS // SEQ_TILE
HALO_BLOCKS = SEQ_TILE // SUBLANES


def _seg_first(t):
    return jnp.logical_or(t >= BATCH * LAT_TILES, t % LAT_TILES == 0)


def _seg_last(t):
    return jnp.logical_or(t >= BATCH * LAT_TILES, t % LAT_TILES == LAT_TILES - 1)


def _conv4(ext_ref, cw_ref, tm, cols):
    ext = ext_ref[:, cols]
    rows = ext.shape[0]
    body = slice(SUBLANES, SUBLANES + tm)
    tap = lambda shift: ext[body] if shift == 0 else pltpu.roll(ext, shift % rows, 0)[body]
    acc = cw_ref[0:1, cols] * tap(2)
    acc = acc + cw_ref[1:2, cols] * tap(1)
    acc = acc + cw_ref[2:3, cols] * tap(0)
    acc = acc + cw_ref[3:4, cols] * tap(-1)
    return acc


def _fill_halo(ext_ref, prev_ref, cur_ref, next_ref, t, tm):
    zero = jnp.zeros((SUBLANES, ext_ref.shape[1]), F32)
    ext_ref[pl.ds(0, SUBLANES), :] = jnp.where(_seg_first(t), zero, prev_ref[...])
    ext_ref[pl.ds(SUBLANES, tm), :] = cur_ref[...]
    ext_ref[pl.ds(SUBLANES + tm, SUBLANES), :] = jnp.where(_seg_last(t), zero, next_ref[...])


def _halo_specs(width, col_of):
    last_blk = T_ROWS // SUBLANES - 1
    return [pl.BlockSpec((SUBLANES, width), lambda t, *g: (jnp.maximum(t * HALO_BLOCKS - 1, 0), col_of(t, *g))),
            pl.BlockSpec((SEQ_TILE, width), lambda t, *g: (t, col_of(t, *g))),
            pl.BlockSpec((SUBLANES, width),
                         lambda t, *g: (jnp.minimum((t + 1) * HALO_BLOCKS, last_blk), col_of(t, *g)))]


def _lru_prep_kernel(py_ref, xp_ref, xc_ref, xn_ref, cw_ref, cb_ref, wg_ref, bg_ref, lam_ref,
                     y_ref, a_ref, u_ref, ext_ref):
    t = pl.program_id(0)
    tm = SEQ_TILE
    y_ref[...] = jax.nn.gelu(py_ref[...])
    _fill_halo(ext_ref, xp_ref, xc_ref, xn_ref, t, tm)
    for n in range(LRU_BLOCKS):
        cols = slice(n * LRU_BW, (n + 1) * LRU_BW)
        xr = _conv4(ext_ref, cw_ref, tm, cols) + cb_ref[:, cols]
        xb = xr.astype(BF16)
        for d in range(2):
            gt = [jnp.dot(xb, wg_ref[d, g, n], preferred_element_type=F32) + bg_ref[d, g, :, cols] for g in range(2)]
            log_a = -LRU_C * jax.nn.sigmoid(gt[0]) * jax.nn.softplus(-lam_ref[d, :, cols])
            a = jnp.exp(log_a)
            a_ref[d, :, cols] = a
            one_minus_a2 = -jnp.tanh(log_a) * (a * a + 1.0)
            u_ref[d, :, cols] = jnp.sqrt(one_minus_a2) * jax.nn.sigmoid(gt[1]) * xr


def lru_prep(p, conv_w, conv_b, w_gate_bf16, b_gate, lam):
    w = LRU_W
    full = lambda shape: pl.BlockSpec(shape, lambda t: (0,) * len(shape))
    return pl.pallas_call(
        _lru_prep_kernel,
        grid=(N_SEQ_TILES,),
        in_specs=[pl.BlockSpec((SEQ_TILE, w), lambda t: (t, 0))] + _halo_specs(w, lambda t: 1) + [
            full((CONV_W, w)), full((1, w)),
            full((2, 2, LRU_BLOCKS, LRU_BW, LRU_BW)), full((2, 2, 1, w)), full((2, 1, w))],
        out_specs=[pl.BlockSpec((SEQ_TILE, w), lambda t: (t, 0)),
                   pl.BlockSpec((2, SEQ_TILE, w), lambda t: (0, t, 0)),
                   pl.BlockSpec((2, SEQ_TILE, w), lambda t: (0, t, 0))],
        out_shape=[jax.ShapeDtypeStruct((T_ROWS, w), F32),
                   jax.ShapeDtypeStruct((2, T_ROWS, w), F32),
                   jax.ShapeDtypeStruct((2, T_ROWS, w), F32)],
        scratch_shapes=[pltpu.VMEM((SEQ_TILE + 2 * SUBLANES, w), F32)],
        compiler_params=_params("arbitrary"),
        name="lru_prep",
    )(p, p, p, p, conv_w, conv_b.reshape(1, w), w_gate_bf16, b_gate.reshape(2, 2, 1, w), lam.reshape(2, 1, w))


def _lru_scan_kernel(a_ref, u_ref, o_ref, carry_ref, *, reverse):
    @pl.when(pl.program_id(2) == 0)
    def _():
        carry_ref[...] = jnp.zeros_like(carry_ref)

    nblk = SEQ_TILE // SUBLANES
    width = a_ref.shape[-1]
    row = lax.broadcasted_iota(jnp.int32, (SUBLANES, width), 0)

    def body(i, carry):
        blk = (nblk - 1 - i) if reverse else i
        r0 = pl.multiple_of(blk * SUBLANES, SUBLANES)
        a = a_ref[0, pl.ds(r0, SUBLANES), :]
        u = u_ref[0, pl.ds(r0, SUBLANES), :]
        for s in (1, 2, 4):
            if reverse:
                keep = row < SUBLANES - s
                shift = SUBLANES - s
            else:
                keep = row >= s
                shift = s
            a_sh = jnp.where(keep, pltpu.roll(a, shift, 0), 1.0)
            u_sh = jnp.where(keep, pltpu.roll(u, shift, 0), 0.0)
            u = a * u_sh + u
            a = a * a_sh
        h = u + a * carry
        o_ref[pl.ds(r0, SUBLANES), :] = h
        return h[0:1, :] if reverse else h[SUBLANES - 1:SUBLANES, :]

    carry_ref[...] = lax.fori_loop(0, nblk, body, carry_ref[...])


def lru_scan(a, u, direction):
    reverse = direction == 1
    tw = LRU_W

    def row_tile(b, k):
        lat = b * LAT_TILES + ((LAT_TILES - k) if reverse else (k - 1))
        return jnp.where(k == 0, BATCH * LAT_TILES + b, lat)

    spec3 = pl.BlockSpec((1, SEQ_TILE, tw), lambda b, l, k: (direction, row_tile(b, k), l))
    return pl.pallas_call(
        functools.partial(_lru_scan_kernel, reverse=reverse),
        grid=(BATCH, LRU_W // tw, LAT_TILES + 1),
        in_specs=[spec3, spec3],
        out_specs=pl.BlockSpec((SEQ_TILE, tw), lambda b, l, k: (row_tile(b, k), l)),
        out_shape=jax.ShapeDtypeStruct((T_ROWS, LRU_W), F32),
        scratch_shapes=[pltpu.VMEM((1, tw), F32)],
        compiler_params=_params("arbitrary", "arbitrary", "arbitrary"),
        name="lru_scan_bwd" if reverse else "lru_scan_fwd",
    )(a, u)


def _lru_post_kernel(hf_ref, hb_ref, y_ref, o_ref):
    o_ref[...] = ((hf_ref[...] + hb_ref[...]) * y_ref[...]).astype(BF16)


def lru_post(hf, hb, y):
    tm = 512
    spec = pl.BlockSpec((tm, LRU_W), lambda t: (t, 0))
    return pl.pallas_call(
        _lru_post_kernel,
        grid=(T_ROWS // tm,),
        in_specs=[spec, spec, spec],
        out_specs=spec,
        out_shape=jax.ShapeDtypeStruct((T_ROWS, LRU_W), BF16),
        compiler_params=_params("arbitrary"),
        name="lru_post",
    )(hf, hb, y)


def lru_mixer(p, conv_w, conv_b, w_gate, b_gate, lam):
    y, a, u = lru_prep(p, conv_w, conv_b, w_gate.astype(BF16), b_gate, lam)
    return lru_post(lru_scan(a, u, 0), lru_scan(a, u, 1), y)


PAIR_SLOTS = 16
NEG_BIG = -1e30


def _split3(x):
    hi = x.astype(BF16)
    r1 = x - hi.astype(F32)
    mid = r1.astype(BF16)
    lo = (r1 - mid.astype(F32)).astype(BF16)
    return hi, mid, lo


def _gates_kernel(x_ref, pa_ref, pb_ref, o_ref, orow_ref, *, kind, npairs):
    x = x_ref[...]
    tm = x.shape[0]
    lane = lax.broadcasted_iota(jnp.int32, x.shape, 1)
    if kind == "gdn":
        first = jax.nn.sigmoid(x)
        second = -jnp.exp(pa_ref[...]) * jax.nn.softplus(x + pb_ref[...])
    else:
        gt = GATE_CAP * jnp.tanh((x + pb_ref[...]) / GATE_CAP)
        first = gt
        second = jax.nn.log_sigmoid(gt)
    r = lax.broadcasted_iota(jnp.int32, (tm, tm), 0)
    s = lax.broadcasted_iota(jnp.int32, (tm, tm), 1)
    same = (r // CHUNK) == (s // CHUNK)
    as_w = lambda m: jnp.where(m, 1.0, 0.0).astype(BF16)
    parts = _split3(second)

    def chunk_sum(w):
        acc = jnp.dot(w, parts[0], preferred_element_type=F32)
        acc = acc + jnp.dot(w, parts[1], preferred_element_type=F32)
        return acc + jnp.dot(w, parts[2], preferred_element_type=F32)

    prefix = chunk_sum(as_w(jnp.logical_and(same, s <= r)))
    suffix = chunk_sum(as_w(jnp.logical_and(same, s >= r)))
    total = chunk_sum(as_w(same))
    cs = jnp.where((lane & 3) >= 2, suffix, prefix)
    x1 = jnp.where((lane & 7) < 4, first, cs)
    row_q = cs if kind == "gdn" else pltpu.roll(first, 4, 1) - cs
    row_t = row_q.T
    for j in range(npairs):
        o_ref[j] = jnp.concatenate([x1[:, 8 * j:8 * j + 8], total[:, 8 * j:8 * j + 8]], axis=1)
        orow_ref[j] = row_t[8 * j + 4:8 * j + 8, :]


def gate_columns(p, col_block, row_a, row_b, kind, npairs):
    return pl.pallas_call(
        functools.partial(_gates_kernel, kind=kind, npairs=npairs),
        grid=(N_SEQ_TILES,),
        in_specs=[pl.BlockSpec((SEQ_TILE, LANES), lambda t: (t, col_block)),
                  pl.BlockSpec((1, LANES), lambda t: (0, 0)),
                  pl.BlockSpec((1, LANES), lambda t: (0, 0))],
        out_specs=[pl.BlockSpec((npairs, SEQ_TILE, PAIR_SLOTS), lambda t: (0, t, 0)),
                   pl.BlockSpec((npairs, 4, SEQ_TILE), lambda t: (0, 0, t))],
        out_shape=[jax.ShapeDtypeStruct((npairs, T_ROWS, PAIR_SLOTS), F32),
                   jax.ShapeDtypeStruct((npairs, 4, T_ROWS), F32)],
        compiler_params=_params("arbitrary"),
        name=kind + "_gates",
    )(p, row_a, row_b)


STACK = 4 * CHUNK
CHUNKS_PER_TILE = SEQ_TILE // CHUNK


def _mm(a, b):
    return jnp.dot(a.astype(BF16), b.astype(BF16), preferred_element_type=F32)


def _mm_nt(a, b):
    return lax.dot_general(a.astype(BF16), b.astype(BF16), (((1,), (1,)), ((), ())),
                           preferred_element_type=F32)


def _stack_masks():
    r = lax.broadcasted_iota(jnp.int32, (STACK, STACK), 0)
    s = lax.broadcasted_iota(jnp.int32, (STACK, STACK), 1)
    same = (r // CHUNK) == (s // CHUNK)
    fwd = r < 2 * CHUNK
    ordered = jnp.logical_or(jnp.logical_and(fwd, s <= r), jnp.logical_and(jnp.logical_not(fwd), s >= r))
    incl = jnp.logical_and(same, ordered)
    strict = jnp.logical_and(incl, r != s)
    return r, s, incl, strict


def _col4(gf, gb, slot):
    return jnp.concatenate([gf[:, slot:slot + 1], gf[:, slot + 1:slot + 2],
                            gb[:, slot + 2:slot + 3], gb[:, slot + 3:slot + 4]], axis=0)


def _diag_blocks(x, rows0, width):
    return jnp.concatenate([x[rows0 + c * CHUNK:rows0 + (c + 1) * CHUNK, c * width:(c + 1) * width]
                            for c in range(4)], axis=0)


def _block_diag_place(v, r_col):
    chain = r_col // CHUNK
    return jnp.concatenate([jnp.where(chain == c, v, 0.0) for c in range(4)], axis=1)


def _chain_scale(col, width):
    blocks = []
    for c in range(4):
        t = jnp.broadcast_to(col[c * CHUNK:(c + 1) * CHUNK], (CHUNK, width))
        blocks.append(jnp.concatenate([t, t], axis=0))
    return jnp.concatenate(blocks, axis=1)


def _state_products(x, y, state, width):
    xs, ys = [], []
    for p in range(2):
        rows = slice(2 * p * CHUNK, (2 * p + 2) * CHUNK)
        out = _mm(jnp.concatenate([x[rows], y[rows]], axis=0), state[:, 2 * p * width:(2 * p + 2) * width])
        for e in range(2):
            cols = slice(e * width, (e + 1) * width)
            xs.append(out[e * CHUNK:(e + 1) * CHUNK, cols])
            ys.append(out[(2 + e) * CHUNK:(3 + e) * CHUNK, cols])
    return jnp.concatenate(xs, axis=0), jnp.concatenate(ys, axis=0)


def _scan_tile(b, k, reverse):
    lat = b * LAT_TILES + ((LAT_TILES - k) if reverse else (k - 1))
    return jnp.where(k == 0, BATCH * LAT_TILES + b, lat)


def _gdn_conv_kernel(xp_ref, xc_ref, xn_ref, cw_ref, o_ref, ext_ref):
    t = pl.program_id(0)
    ct = pl.program_id(1)
    _fill_halo(ext_ref, xp_ref, xc_ref, xn_ref, t, SEQ_TILE)
    qk_tiles = GDN_QK // GDN_CONV_TILE

    def conv_silu(h):
        cols = slice(h * GDN_DK, (h + 1) * GDN_DK)
        y = _conv4(ext_ref, cw_ref, SEQ_TILE, cols)
        return cols, y * jax.nn.sigmoid(y)

    @pl.when(ct >= 2 * qk_tiles)
    def _():
        for h in range(GDN_CONV_TILE // GDN_DK):
            cols, y = conv_silu(h)
            o_ref[:, cols] = y

    @pl.when(ct < 2 * qk_tiles)
    def _():
        scale = jnp.where(ct < qk_tiles, GDN_DK ** -0.5, 1.0)
        for h in range(GDN_CONV_TILE // GDN_DK):
            cols, y = conv_silu(h)
            nrm = y * lax.rsqrt(jnp.sum(y * y, axis=-1, keepdims=True) + EPS)
            o_ref[:, cols] = nrm * scale


GDN_CONV_TILE = 1024


def gdn_conv(p, conv_w):
    n = 2 * GDN_QK + GDN_VW
    tn = GDN_CONV_TILE
    return pl.pallas_call(
        _gdn_conv_kernel,
        grid=(N_SEQ_TILES, n // tn),
        in_specs=_halo_specs(tn, lambda t, c: c) + [pl.BlockSpec((CONV_W, tn), lambda t, c: (0, c))],
        out_specs=pl.BlockSpec((SEQ_TILE, tn), lambda t, c: (t, c)),
        out_shape=jax.ShapeDtypeStruct((T_ROWS, n), F32),
        scratch_shapes=[pltpu.VMEM((SEQ_TILE + 2 * SUBLANES, tn), F32)],
        compiler_params=_params("arbitrary", "arbitrary"),
        name="gdn_conv",
    )(p, p, p, conv_w)


def _row4(rf_ref, rb_ref, hp, rf, rb):
    return jnp.concatenate([rf_ref[hp, 0:1, rf:rf + CHUNK], rf_ref[hp, 1:2, rf:rf + CHUNK],
                            rb_ref[hp, 2:3, rb:rb + CHUNK], rb_ref[hp, 3:4, rb:rb + CHUNK]], axis=1)


GDN_STEP_PAIRS = 2
GDN_GROUP_CHUNKS = 1
GDN_STAGE_LAG = 2


def _gdn_core_kernel(qf_ref, kf_ref, vf_ref, gf_ref, grf_ref, qb_ref, kb_ref, vb_ref, gb_ref, grb_ref,
                     of_ref, ob_ref, s_ref):
    @pl.when(pl.program_id(2) == 0)
    def _():
        s_ref[...] = jnp.zeros_like(s_ref)

    r, s, incl, strict = _stack_masks()
    eye = r == s
    blk16 = (r // 16) == (s // 16)
    blk32 = (r // 32) == (s // 32)
    r_col = lax.broadcasted_iota(jnp.int32, (STACK, 1), 0)

    pairs = range(GDN_STEP_PAIRS)
    steps = range(CHUNKS_PER_TILE)
    rows_f = [n * CHUNK for n in steps]
    rows_b = [(CHUNKS_PER_TILE - 1 - n) * CHUNK for n in steps]
    problems = [(hp, n) for n in steps for hp in pairs]
    each = lambda fn, *lists: [fn(*xs) for xs in zip(*lists)]

    def load(hp, n):
        rf, rb = rows_f[n], rows_b[n]
        dk = slice(hp * GDN_DK, (hp + 1) * GDN_DK)
        v_of = lambda e: slice((2 * hp + e) * GDN_DV, (2 * hp + e + 1) * GDN_DV)
        kf = kf_ref[pl.ds(rf, CHUNK), dk]
        kb = kb_ref[pl.ds(rb, CHUNK), dk]
        qf = qf_ref[pl.ds(rf, CHUNK), dk]
        qb = qb_ref[pl.ds(rb, CHUNK), dk]
        k4 = jnp.concatenate([kf, kf, kb, kb], axis=0)
        q4 = jnp.concatenate([qf, qf, qb, qb], axis=0)
        v4 = jnp.concatenate([vf_ref[pl.ds(rf, CHUNK), v_of(0)], vf_ref[pl.ds(rf, CHUNK), v_of(1)],
                              vb_ref[pl.ds(rb, CHUNK), v_of(0)], vb_ref[pl.ds(rb, CHUNK), v_of(1)]], axis=0)
        gf = gf_ref[hp, pl.ds(rf, CHUNK), :]
        gb = gb_ref[hp, pl.ds(rb, CHUNK), :]
        return (k4, q4, v4, _col4(gf, gb, 0), _col4(gf, gb, 4), _col4(gf, gb, 12),
                _row4(grf_ref, grb_ref, hp, rf, rb))

    def gram_of(k, q):
        f, b = slice(0, CHUNK), slice(2 * CHUNK, 3 * CHUNK)
        g = _mm_nt(jnp.concatenate([k[f], k[b], q[f], q[b]], axis=0), k)
        return jnp.concatenate([g[i * CHUNK:(i + 1) * CHUNK] for i in (0, 0, 1, 1, 2, 2, 3, 3)], axis=0)

    def merge_fn(blk, level_mask):
        half = blk // 2
        starts = range(0, STACK, blk)
        active = [s0 + (half if s0 < 2 * CHUNK else 0) for s0 in starts]

        def left(a, x):
            rows = jnp.concatenate([a[s0:s0 + half] for s0 in active], axis=0)
            return _mm(rows, jnp.where(level_mask, x, 0.0))

        def right(a, y):
            z = _mm(y, a)
            out = []
            for i, s0 in enumerate(starts):
                upd = a[active[i]:active[i] + half] - z[i * half:(i + 1) * half]
                keep = a[s0:s0 + half] if active[i] != s0 else a[s0 + half:s0 + blk]
                out += [keep, upd] if active[i] != s0 else [upd, keep]
            return jnp.concatenate(out, axis=0)

        return left, right

    merge32 = merge_fn(32, jnp.logical_and(blk32, jnp.logical_not(blk16)))
    merge64 = merge_fn(CHUNK, jnp.logical_not(blk32))

    def phase_a(chunks):
        idx = [(hp, n) for n in chunks for hp in pairs]
        v = {}

        def s_load():
            v["k4"], v["q4"], v["v4"], v["beta"], v["gc"], v["gl"], v["gc_row"] = zip(*[load(hp, n) for hp, n in idx])
            v["eg"] = each(jnp.exp, v["gc"])
            v["gram"] = each(gram_of, v["k4"], v["q4"])

        def s_nmat():
            decay = each(lambda g, g_row: jnp.exp(jnp.where(incl, g - g_row, NEG_BIG)), v["gc"], v["gc_row"])
            v["nmat"] = each(lambda gm, d, b: jnp.where(strict, gm[:STACK] * d * b, 0.0),
                             v["gram"], decay, v["beta"])
            v["a_qk"] = each(lambda gm, d: gm[STACK:] * d, v["gram"], decay)
            v["n0"] = each(lambda x: jnp.where(blk16, x, 0.0), v["nmat"])
            v["t"] = each(lambda x: jnp.where(eye, 1.0, 0.0) - x, v["n0"])

        def s_p2():
            v["p2"] = each(_mm, v["n0"], v["n0"])

        def s_p4():
            v["p4"] = each(_mm, v["p2"], v["p2"])
            v["t"] = each(lambda a, b: a + _mm(a, b), v["t"], v["p2"])

        def s_p8():
            v["p8"] = each(_mm, v["p4"], v["p4"])
            v["t"] = each(lambda a, b: a + _mm(a, b), v["t"], v["p4"])

        def s_t16():
            v["t"] = each(lambda a, b: a + _mm(a, b), v["t"], v["p8"])

        def s_left(merge):
            return lambda: v.__setitem__("y", each(merge[0], v["t"], v["nmat"]))

        def s_right(merge):
            return lambda: v.__setitem__("t", each(merge[1], v["t"], v["y"]))

        def s_uw():
            v["uw"] = each(lambda a, x, k, b, e: _mm(a, jnp.concatenate([x * b, k * (b * e)], axis=1)),
                           v["t"], v["v4"], v["k4"], v["beta"], v["eg"])
            v["qe"] = each(lambda q, e: q * e, v["q4"], v["eg"])
            v["k_t"] = each(lambda k, g, gt: (k * jnp.exp(gt - g)).T, v["k4"], v["gc"], v["gl"])
            v["s_scale"] = each(lambda gt: _chain_scale(jnp.exp(gt), GDN_DV), v["gl"])

        stages = [s_load, s_nmat, s_p2, s_p4, s_p8, s_t16, s_left(merge32), s_right(merge32),
                  s_left(merge64), s_right(merge64), s_uw]
        return v, stages, {n: i * GDN_STEP_PAIRS for i, n in enumerate(chunks)}

    s_all = [s_ref[hp] for hp in pairs]

    def phase_b(group, n):
        v, _, base = group
        tmp = {}

        def s_state():
            for hp in pairs:
                i = base[n] + hp
                ws, qs = _state_products(v["uw"][i][:, GDN_DV:], v["qe"][i], s_all[hp], GDN_DV)
                tmp[hp] = (v["uw"][i][:, :GDN_DV] - ws, qs)

        def s_update():
            for hp in pairs:
                i = base[n] + hp
                v_new, qs = tmp[hp]
                o4 = qs + _mm(v["a_qk"][i], v_new)
                s_all[hp] = s_all[hp] * v["s_scale"][i] + _mm(v["k_t"][i], _block_diag_place(v_new, r_col))
                rf, rb = rows_f[n], rows_b[n]
                for c, (o_ref, r0) in enumerate(((of_ref, rf), (of_ref, rf), (ob_ref, rb), (ob_ref, rb))):
                    col = (2 * hp + c % 2) * GDN_DV
                    o_ref[pl.ds(r0, CHUNK), col:col + GDN_DV] = o4[c * CHUNK:(c + 1) * CHUNK]

        return [s_state, s_update]

    slots = {}
    for g in range(0, CHUNKS_PER_TILE, GDN_GROUP_CHUNKS):
        chunks = list(steps)[g:g + GDN_GROUP_CHUNKS]
        group = phase_a(chunks)
        sequence = group[1] + [st for n in chunks for st in phase_b(group, n)]
        for k, st in enumerate(sequence):
            slots.setdefault((g // GDN_GROUP_CHUNKS) * GDN_STAGE_LAG + k, []).append(st)
    for slot in sorted(slots):
        for st in slots[slot]:
            st()
    for hp in pairs:
        s_ref[hp] = s_all[hp]


def gdn_core(qkv, gcols, grows):
    hp = GDN_STEP_PAIRS
    kcol = GDN_QK // (hp * GDN_DK)
    vcol = 2 * GDN_QK // (hp * 2 * GDN_DV)

    def specs(reverse):
        tile = lambda b, j, k: _scan_tile(b, k, reverse)
        return [pl.BlockSpec((SEQ_TILE, hp * GDN_DK), lambda b, j, k: (tile(b, j, k), j)),
                pl.BlockSpec((SEQ_TILE, hp * GDN_DK), lambda b, j, k: (tile(b, j, k), kcol + j)),
                pl.BlockSpec((SEQ_TILE, hp * 2 * GDN_DV), lambda b, j, k: (tile(b, j, k), vcol + j)),
                pl.BlockSpec((hp, SEQ_TILE, PAIR_SLOTS), lambda b, j, k: (j, tile(b, j, k), 0)),
                pl.BlockSpec((hp, 4, SEQ_TILE), lambda b, j, k: (j, 0, tile(b, j, k)))]

    out_spec = lambda reverse: pl.BlockSpec((SEQ_TILE, hp * 2 * GDN_DV),
                                            lambda b, j, k: (_scan_tile(b, k, reverse), j))
    return pl.pallas_call(
        _gdn_core_kernel,
        grid=(BATCH, GDN_QK_HEADS // hp, LAT_TILES + 1),
        in_specs=specs(False) + specs(True),
        out_specs=[out_spec(False), out_spec(True)],
        out_shape=[jax.ShapeDtypeStruct((T_ROWS, GDN_VW), F32)] * 2,
        scratch_shapes=[pltpu.VMEM((hp, GDN_DK, 4 * GDN_DV), F32)],
        compiler_params=_params("arbitrary", "arbitrary", "arbitrary"),
        name="gdn_core",
    )(qkv, qkv, qkv, gcols, grows, qkv, qkv, qkv, gcols, grows)


def _head_norm_gate_kernel(of_ref, ob_ref, z_ref, nw_ref, y_ref, *, head_dim, gate):
    for h in range(of_ref.shape[1] // head_dim):
        sl = slice(h * head_dim, (h + 1) * head_dim)
        o = of_ref[:, sl] + ob_ref[:, sl]
        y = o * lax.rsqrt(jnp.mean(o * o, axis=-1, keepdims=True) + EPS) * nw_ref[:, sl]
        z = z_ref[:, sl]
        g = z * jax.nn.sigmoid(z) if gate == "silu" else jax.nn.sigmoid(z)
        y_ref[:, sl] = (y * g).astype(BF16)


def head_norm_gate(o_f, o_b, p, z_col_block, norm_row, head_dim, gate):
    width = o_f.shape[1]
    tm, tn = 512, 512
    spec = pl.BlockSpec((tm, tn), lambda t, c: (t, c))
    return pl.pallas_call(
        functools.partial(_head_norm_gate_kernel, head_dim=head_dim, gate=gate),
        grid=(T_ROWS // tm, width // tn),
        in_specs=[spec, spec, pl.BlockSpec((tm, tn), lambda t, c: (t, z_col_block + c)),
                  pl.BlockSpec((1, tn), lambda t, c: (0, c))],
        out_specs=spec,
        out_shape=jax.ShapeDtypeStruct((T_ROWS, width), BF16),
        compiler_params=_params("arbitrary", "arbitrary"),
        name="head_norm_" + gate,
    )(o_f, o_b, p, norm_row)


def gdn_mixer(p, conv_w, a_log, dt_bias, norm_w):
    zeros4 = jnp.zeros((GDN_QK_HEADS, 4), F32)
    per_pair = lambda t: t.reshape(2, GDN_QK_HEADS, 2).transpose(1, 0, 2).reshape(GDN_QK_HEADS, 4)
    row = lambda t: jnp.concatenate([zeros4, per_pair(t)], axis=1).reshape(1, LANES)
    gcols, grows = gate_columns(p, (2 * GDN_QK + 2 * GDN_VW) // LANES, row(a_log), row(dt_bias), "gdn",
                                GDN_QK_HEADS)
    qkv = gdn_conv(p, conv_w)
    o_f, o_b = gdn_core(qkv, gcols, grows)
    norm_row = jnp.tile(norm_w, GDN_V_HEADS).reshape(1, GDN_VW)
    return head_norm_gate(o_f, o_b, p, (2 * GDN_QK + GDN_VW) // 512, norm_row, GDN_DV, "silu")


def gdn_in_weight(w_in):
    base = 2 * GDN_QK + 2 * GDN_VW
    gates = w_in[:, base:].reshape(D_MODEL, 2, 2, GDN_QK_HEADS, 2)
    gates = gates.transpose(0, 3, 1, 2, 4).reshape(D_MODEL, 4 * GDN_V_HEADS)
    return _pad_cols(jnp.concatenate([w_in[:, :base], gates], axis=1), COL_TILE)


ML_PAIRS = ML_HEADS // 2


def _mlstm_core_kernel(qf_ref, kf_ref, vf_ref, gf_ref, grf_ref, qb_ref, kb_ref, vb_ref, gb_ref, grb_ref,
                       of_ref, ob_ref, c_ref, n_ref, m_ref):
    @pl.when(pl.program_id(2) == 0)
    def _():
        c_ref[...] = jnp.zeros_like(c_ref)
        n_ref[...] = jnp.zeros_like(n_ref)
        m_ref[...] = jnp.zeros_like(m_ref)

    r, s, incl, _ = _stack_masks()
    r_col = lax.broadcasted_iota(jnp.int32, (STACK, 1), 0)
    last_rows = (CHUNK - 1, 2 * CHUNK - 1, 2 * CHUNK, 3 * CHUNK)

    steps = range(CHUNKS_PER_TILE)
    rows_f = [n * CHUNK for n in steps]
    rows_b = [(CHUNKS_PER_TILE - 1 - n) * CHUNK for n in steps]
    each = lambda fn, *lists: [fn(*xs) for xs in zip(*lists)]

    def load(rf, rb):
        def stack(f_ref, b_ref, w):
            return jnp.concatenate([f_ref[pl.ds(rf, CHUNK), 0:w], f_ref[pl.ds(rf, CHUNK), w:2 * w],
                                    b_ref[pl.ds(rb, CHUNK), 0:w], b_ref[pl.ds(rb, CHUNK), w:2 * w]], axis=0)

        gf = gf_ref[0, pl.ds(rf, CHUNK), :]
        gb = gb_ref[0, pl.ds(rb, CHUNK), :]
        return (stack(qf_ref, qb_ref, ML_DQK) * ML_DQK ** -0.5, stack(kf_ref, kb_ref, ML_DQK),
                stack(vf_ref, vb_ref, ML_DV), _col4(gf, gb, 0), _col4(gf, gb, 4), _col4(gf, gb, 12),
                _row4(grf_ref, grb_ref, 0, rf, rb))

    q4, k4, v4, ig, bc, bl, ib_row = zip(*each(load, rows_f, rows_b))
    d_log = each(lambda b, row: jnp.where(incl, b + row, NEG_BIG), bc, ib_row)
    row_max = each(lambda d: jnp.max(d, axis=1, keepdims=True), d_log)
    w_max = each(lambda rm: jnp.concatenate([jnp.broadcast_to(rm[i:i + 1], (CHUNK, 1)) for i in last_rows], axis=0),
                 row_max)
    qk = each(_mm_nt, q4, k4)

    m_in = [m_ref[...]]
    for n in steps:
        m_in.append(jnp.maximum(bl[n] + m_in[n], w_max[n]))
    m_out = m_in[1:]
    m_ref[...] = m_in[-1]

    inter = each(lambda b, m: b + m, bc, m_in)
    m_t = each(jnp.maximum, inter, row_max)
    smat = each(lambda x, d, mt: x * jnp.exp(d - mt), qk, d_log, m_t)
    w_inter = each(lambda i, mt: jnp.exp(i - mt), inter, m_t)
    intra = each(_mm, smat, v4)
    s_sum = each(lambda x: jnp.sum(x, axis=1, keepdims=True), smat)
    cd = each(lambda b, m0, m1: jnp.exp(b + m0 - m1), bl, m_in, m_out)
    kw = each(lambda k, b, c, i, m1: k * jnp.exp(b - c + i - m1), k4, bl, bc, ig, m_out)
    d_c = each(lambda x, v: _mm(x.T, _block_diag_place(v, r_col)), kw, v4)
    c_scale = each(lambda x: _chain_scale(x, ML_DV), cd)

    c_all = c_ref[...]
    n_all = [n_ref[c:c + 1, :] for c in range(4)]
    for n in steps:
        qc = []
        for p in range(2):
            out = _mm(q4[n][2 * p * CHUNK:(2 * p + 2) * CHUNK], c_all[:, 2 * p * ML_DV:(2 * p + 2) * ML_DV])
            qc += [out[e * CHUNK:(e + 1) * CHUNK, e * ML_DV:(e + 1) * ML_DV] for e in range(2)]
        qc = jnp.concatenate(qc, axis=0)
        num = w_inter[n] * qc + intra[n]
        n_rows = jnp.concatenate([jnp.broadcast_to(x, (CHUNK, ML_DQK)) for x in n_all], axis=0)
        den = w_inter[n] * jnp.sum(q4[n] * n_rows, axis=1, keepdims=True) + s_sum[n]
        h4 = num / jnp.maximum(jnp.abs(den), jnp.exp(-m_t[n]))
        c_all = c_all * c_scale[n] + d_c[n]
        n_all = [cd[n][c * CHUNK:c * CHUNK + 1] * n_all[c]
                 + jnp.sum(kw[n][c * CHUNK:(c + 1) * CHUNK], axis=0, keepdims=True) for c in range(4)]
        rf, rb = rows_f[n], rows_b[n]
        of_ref[pl.ds(rf, CHUNK), 0:ML_DV] = h4[0:CHUNK]
        of_ref[pl.ds(rf, CHUNK), ML_DV:2 * ML_DV] = h4[CHUNK:2 * CHUNK]
        ob_ref[pl.ds(rb, CHUNK), 0:ML_DV] = h4[2 * CHUNK:3 * CHUNK]
        ob_ref[pl.ds(rb, CHUNK), ML_DV:2 * ML_DV] = h4[3 * CHUNK:4 * CHUNK]
    c_ref[...] = c_all
    for c in range(4):
        n_ref[c:c + 1, :] = n_all[c]


def mlstm_core(p, gcols, grows):
    kcol = ML_QK // (2 * ML_DQK)
    vcol = 2 * ML_QK // (2 * ML_DV)

    def specs(reverse):
        tile = lambda b, j, k: _scan_tile(b, k, reverse)
        return [pl.BlockSpec((SEQ_TILE, 2 * ML_DQK), lambda b, j, k: (tile(b, j, k), j)),
                pl.BlockSpec((SEQ_TILE, 2 * ML_DQK), lambda b, j, k: (tile(b, j, k), kcol + j)),
                pl.BlockSpec((SEQ_TILE, 2 * ML_DV), lambda b, j, k: (tile(b, j, k), vcol + j)),
                pl.BlockSpec((1, SEQ_TILE, PAIR_SLOTS), lambda b, j, k: (j, tile(b, j, k), 0)),
                pl.BlockSpec((1, 4, SEQ_TILE), lambda b, j, k: (j, 0, tile(b, j, k)))]

    out_spec = lambda reverse: pl.BlockSpec((SEQ_TILE, 2 * ML_DV),
                                            lambda b, j, k: (_scan_tile(b, k, reverse), j))
    return pl.pallas_call(
        _mlstm_core_kernel,
        grid=(BATCH, ML_PAIRS, LAT_TILES + 1),
        in_specs=specs(False) + specs(True),
        out_specs=[out_spec(False), out_spec(True)],
        out_shape=[jax.ShapeDtypeStruct((T_ROWS, ML_V), F32)] * 2,
        scratch_shapes=[pltpu.VMEM((ML_DQK, 4 * ML_DV), F32), pltpu.VMEM((SUBLANES, ML_DQK), F32),
                        pltpu.VMEM((STACK, 1), F32)],
        compiler_params=_params("arbitrary", "arbitrary", "arbitrary"),
        name="mlstm_core",
    )(p, p, p, gcols, grows, p, p, p, gcols, grows)


def mlstm_mixer(p, gate_b, norm_w):
    bias = gate_b.reshape(2, 2, ML_PAIRS, 2).transpose(2, 1, 0, 3).reshape(1, 4 * ML_HEADS)
    bias = jnp.pad(bias, ((0, 0), (0, LANES - 4 * ML_HEADS)))
    gcols, grows = gate_columns(p, (2 * ML_QK + 2 * ML_V) // LANES, bias, bias, "mlstm", ML_PAIRS)
    h_f, h_b = mlstm_core(p, gcols, grows)
    return head_norm_gate(h_f, h_b, p, (2 * ML_QK + ML_V) // 512, norm_w.reshape(1, ML_V), ML_DV, "sigmoid")


def mlstm_in_weight(w_in):
    base = 2 * ML_QK + 2 * ML_V
    gates = w_in[:, base:].reshape(D_MODEL, 2, 2, ML_PAIRS, 2)
    gates = gates.transpose(0, 3, 2, 1, 4).reshape(D_MODEL, 4 * ML_HEADS)
    return _pad_cols(jnp.concatenate([w_in[:, :base], gates], axis=1), COL_TILE)


def _pad_cols(w, mult):
    n = w.shape[1]
    return jnp.pad(w, ((0, 0), (0, (-n) % mult)))


def kernel(x, c, ctx, c_ctx, w_mod, b_mod, norm_mix, norm_ff, norm_out, gdn_w_in, gdn_conv, gdn_a_log, gdn_dt_bias, gdn_norm, gdn_w_out, ml_w_in, ml_gate_b, ml_norm, ml_w_out, lru_w_in, lru_conv, lru_conv_b, lru_w_gate, lru_b_gate, lru_lambda, lru_w_out, ff_w_up, ff_w_down):
    assert DEPTH % 2 == 0
    r_lat = x.reshape(N_LAT, D_MODEL)
    r_ctx = ctx.reshape(N_CTX, D_MODEL)
    c_all = jnp.concatenate([c, c_ctx[None], jnp.zeros((MOD_ROWS - BATCH - 1, D_MODEL), F32)], axis=0)
    mod = modulation_table(c_all, w_mod, b_mod)
    for i in range(DEPTH):
        kind, j = i % N_MIXERS, i // N_MIXERS
        if kind == 0:
            p = in_proj(r_lat, r_ctx, norm_mix[i], mod[i], gdn_in_weight(gdn_w_in[j]).astype(BF16))
            y = gdn_mixer(p, gdn_conv[j], gdn_a_log[j], gdn_dt_bias[j], gdn_norm[j])
            w_out = gdn_w_out[j]
        elif kind == 1:
            p = in_proj(r_lat, r_ctx, norm_mix[i], mod[i], mlstm_in_weight(ml_w_in[j]).astype(BF16))
            y = mlstm_mixer(p, ml_gate_b[j], ml_norm[j])
            w_out = ml_w_out[j]
        else:
            p = in_proj(r_lat, r_ctx, norm_mix[i], mod[i], lru_w_in[j].astype(BF16))
            y = lru_mixer(p, lru_conv[j], lru_conv_b[j], lru_w_gate[j], lru_b_gate[j], lru_lambda[j])
            w_out = lru_w_out[j]
        w_out, w_up, w_down = w_out.astype(BF16), ff_w_up[i].astype(BF16), ff_w_down[i].astype(BF16)
        r_lat = out_proj(y, w_out, r_lat, mod[i], is_ctx=False)
        r_lat = mlp(r_lat, norm_ff[i], mod[i], w_up, w_down, is_ctx=False)
        if i < DEPTH - 1:
            r_ctx = out_proj(y, w_out, r_ctx, mod[i], is_ctx=True)
            r_ctx = mlp(r_ctx, norm_ff[i], mod[i], w_up, w_down, is_ctx=True)
    return final_norm(r_lat, norm_out).reshape(BATCH, SEQ, D_MODEL)
```

```python
import functools
import math

import jax
import jax.numpy as jnp
from jax import lax
from jax.experimental import pallas as pl
from jax.experimental.pallas import tpu as pltpu

F32 = jnp.float32
BF16 = jnp.bfloat16

D_MODEL = 2048
BATCH = 4
SEQ = 4096
DEPTH = 4
CTX_LEN = 256
GRID_W = 64
N_MIXERS = 3
CHUNK = 64
CONV_W = 4
EPS = 1e-6
D_FF = 4 * D_MODEL

GDN_QK_HEADS = D_MODEL // 128
GDN_V_HEADS = 2 * GDN_QK_HEADS
GDN_DK = 128
GDN_DV = 128
GDN_QK = GDN_QK_HEADS * GDN_DK
GDN_VW = GDN_V_HEADS * GDN_DV
GDN_REP = GDN_V_HEADS // GDN_QK_HEADS
GDN_IN = 2 * GDN_QK + 2 * GDN_VW + 4 * GDN_V_HEADS

ML_HEADS = 8
ML_DQK = D_MODEL // (2 * ML_HEADS)
ML_DV = D_MODEL // ML_HEADS
ML_QK = ML_HEADS * ML_DQK
ML_V = ML_HEADS * ML_DV
ML_IN = 2 * ML_QK + 2 * ML_V + 4 * ML_HEADS
GATE_CAP = 15.0

LRU_W = D_MODEL
LRU_BLOCKS = 8
LRU_BW = LRU_W // LRU_BLOCKS
LRU_C = 8.0

N_LAT = BATCH * SEQ
N_CTX = BATCH * CTX_LEN
T_ROWS = N_LAT + N_CTX
MOD_ROWS = 8
CTX_MOD_ROW = BATCH

V7X_VMEM_BYTES = 64 * 1024 * 1024
VMEM_LIMIT = 56 * 1024 * 1024
SUBLANES = 8
LANES = 128

ROW_TILE = 1024
MLP_ROW_TILE = 512
COL_TILE = 512
IN_PROJ_MAX_COL_TILE = 1280
SEQ_TILE = 256


def _params(*sem):
    return pltpu.CompilerParams(dimension_semantics=sem, vmem_limit_bytes=VMEM_LIMIT)


def _mod_row(tile, tile_rows):
    lat_tiles = N_LAT // tile_rows
    return jnp.where(tile < lat_tiles, tile // (SEQ // tile_rows), CTX_MOD_ROW)


def _mod_kernel(c_ref, w_ref, b_ref, o_ref):
    c = c_ref[...]
    s = c * jax.nn.sigmoid(c)
    o_ref[0] = jnp.dot(s.astype(BF16), w_ref[0].astype(BF16), preferred_element_type=F32) + b_ref[0]


def modulation_table(c_all, w_mod, b_mod):
    tn = 1024
    n6 = 6 * D_MODEL
    return pl.pallas_call(
        _mod_kernel,
        grid=(DEPTH, n6 // tn),
        in_specs=[pl.BlockSpec((MOD_ROWS, D_MODEL), lambda i, n: (0, 0)),
                  pl.BlockSpec((1, D_MODEL, tn), lambda i, n: (i, 0, n)),
                  pl.BlockSpec((1, 1, tn), lambda i, n: (i, 0, n))],
        out_specs=pl.BlockSpec((1, MOD_ROWS, tn), lambda i, n: (i, 0, n)),
        out_shape=jax.ShapeDtypeStruct((DEPTH, MOD_ROWS, n6), F32),
        compiler_params=_params("arbitrary", "arbitrary"),
        name="modulation_table",
    )(c_all, w_mod, b_mod.reshape(DEPTH, 1, n6))


def _norm_modulate(x, nw, shift, scale):
    y = x * lax.rsqrt(jnp.mean(x * x, axis=-1, keepdims=True) + EPS)
    return (y * nw) * (1.0 + scale) + shift


def _in_proj_kernel(xl_ref, xc_ref, nw_ref, sh_ref, sc_ref, w_ref, wg_ref, o_ref, og_ref, h_ref, *, tm):
    t = pl.program_id(0)
    row = _mod_row(t, tm)
    lat_tiles = N_LAT // tm

    def prologue(x_ref):
        h = _norm_modulate(x_ref[...], nw_ref[...], sh_ref[pl.ds(row, 1), :], sc_ref[pl.ds(row, 1), :])
        h_ref[...] = h.astype(BF16)
        og_ref[...] = jnp.dot(h_ref[...], wg_ref[...], preferred_element_type=F32)

    @pl.when(jnp.logical_and(pl.program_id(1) == 0, t < lat_tiles))
    def _():
        prologue(xl_ref)

    @pl.when(jnp.logical_and(pl.program_id(1) == 0, t >= lat_tiles))
    def _():
        prologue(xc_ref)

    o_ref[...] = jnp.dot(h_ref[...], w_ref[...], preferred_element_type=F32)


def in_proj(r_lat, r_ctx, norm_w, mod_i, w_bf16, w_gates_bf16):
    n = w_bf16.shape[1]
    tm = ROW_TILE
    tn = max(d * LANES for d in range(1, IN_PROJ_MAX_COL_TILE // LANES + 1) if (n // LANES) % d == 0)
    lat_tiles = N_LAT // tm
    assert N_CTX == tm and n % LANES == 0
    once = pl.Buffered(1)
    return pl.pallas_call(
        functools.partial(_in_proj_kernel, tm=tm),
        grid=(T_ROWS // tm, n // tn),
        in_specs=[pl.BlockSpec((tm, D_MODEL), lambda t, j: (jnp.minimum(t, lat_tiles - 1), 0), pipeline_mode=once),
                  pl.BlockSpec((tm, D_MODEL), lambda t, j: (0, 0), pipeline_mode=once),
                  pl.BlockSpec((1, D_MODEL), lambda t, j: (0, 0)),
                  pl.BlockSpec((MOD_ROWS, D_MODEL), lambda t, j: (0, 0)),
                  pl.BlockSpec((MOD_ROWS, D_MODEL), lambda t, j: (0, 1)),
                  pl.BlockSpec((D_MODEL, tn), lambda t, j: (0, j)),
                  pl.BlockSpec((D_MODEL, LANES), lambda t, j: (0, 0))],
        out_specs=[pl.BlockSpec((tm, tn), lambda t, j: (t, j)),
                   pl.BlockSpec((tm, LANES), lambda t, j: (t, 0))],
        out_shape=[jax.ShapeDtypeStruct((T_ROWS, n), F32), jax.ShapeDtypeStruct((T_ROWS, LANES), F32)],
        scratch_shapes=[pltpu.VMEM((tm, D_MODEL), BF16)],
        compiler_params=_params("arbitrary", "arbitrary"),
        name="in_proj",
    )(r_lat, r_ctx, norm_w.reshape(1, D_MODEL), mod_i, mod_i, w_bf16, w_gates_bf16)


def _segment_mod_row(tile, tile_rows, is_ctx):
    return CTX_MOD_ROW if is_ctx else tile // (SEQ // tile_rows)


def _out_proj_kernel(y_ref, w_ref, r_ref, g_ref, o_ref, *, tm, is_ctx):
    row = _segment_mod_row(pl.program_id(0), tm, is_ctx)
    acc = jnp.dot(y_ref[...], w_ref[...], preferred_element_type=F32)
    o_ref[...] = r_ref[...] + g_ref[pl.ds(row, 1), :] * acc


def out_proj(y_bf16, w_bf16, r_seg, mod_i, is_ctx):
    k = y_bf16.shape[1]
    tm, tn = ROW_TILE, COL_TILE
    nj = D_MODEL // tn
    rows = r_seg.shape[0]
    tile0 = N_LAT // tm if is_ctx else 0
    return pl.pallas_call(
        functools.partial(_out_proj_kernel, tm=tm, is_ctx=is_ctx),
        grid=(rows // tm, nj),
        in_specs=[pl.BlockSpec((tm, k), lambda t, j: (tile0 + t, 0)),
                  pl.BlockSpec((k, tn), lambda t, j: (0, j)),
                  pl.BlockSpec((tm, tn), lambda t, j: (t, j)),
                  pl.BlockSpec((MOD_ROWS, tn), lambda t, j: (0, 2 * nj + j))],
        out_specs=pl.BlockSpec((tm, tn), lambda t, j: (t, j)),
        out_shape=jax.ShapeDtypeStruct((rows, D_MODEL), F32),
        compiler_params=_params("arbitrary", "arbitrary"),
        name="out_proj_ctx" if is_ctx else "out_proj",
    )(y_bf16, w_bf16, r_seg, mod_i)


def _mlp_kernel(x_ref, nw_ref, sh_ref, sc_ref, g_ref, wu_ref, wd_ref, o_ref, h_ref, acc_ref, *maybe_tr_ref,
                tm, is_ctx):
    tr_ref = None if is_ctx else maybe_tr_ref[0]
    f = pl.program_id(1)
    row = _segment_mod_row(pl.program_id(0), tm, is_ctx)

    @pl.when(f == 0)
    def _():
        h = _norm_modulate(x_ref[...], nw_ref[...], sh_ref[pl.ds(row, 1), :], sc_ref[pl.ds(row, 1), :])
        h_ref[...] = h.astype(BF16)
        acc_ref[...] = jnp.zeros_like(acc_ref)

    u = jnp.dot(h_ref[...], wu_ref[...], preferred_element_type=F32)
    a = jnp.square(jnp.maximum(u, 0.0))
    acc_ref[...] += jnp.dot(a.astype(BF16), wd_ref[...], preferred_element_type=F32)

    @pl.when(f == pl.num_programs(1) - 1)
    def _():
        res = x_ref[...] + g_ref[pl.ds(row, 1), :] * acc_ref[...]
        if is_ctx:
            o_ref[...] = res
        else:
            lane_tiles = D_MODEL // LANES
            for c in range(lane_tiles):
                tr_ref[c] = res[:, c * LANES:(c + 1) * LANES]
            for b in range(GRID_W):
                for c in range(lane_tiles):
                    o_ref[b, :, c * LANES:(c + 1) * LANES] = tr_ref.at[c][pl.ds(b, tm // GRID_W, stride=GRID_W), :]


def mlp(r_seg, norm_w, mod_i, wu_bf16, wd_bf16, is_ctx):
    tm, tf = MLP_ROW_TILE, 1024
    rows = r_seg.shape[0]
    if is_ctx:
        out_spec = pl.BlockSpec((tm, D_MODEL), lambda t, f: (t, 0))
        out_shape = jax.ShapeDtypeStruct((rows, D_MODEL), F32)
    else:
        assert SEQ == GRID_W * GRID_W and tm % GRID_W == 0
        groups = tm // GRID_W
        per_batch = GRID_W // groups
        out_spec = pl.BlockSpec((GRID_W, groups, D_MODEL), lambda t, f: (t // per_batch, t % per_batch, 0))
        out_shape = jax.ShapeDtypeStruct((rows // GRID_W, GRID_W, D_MODEL), F32)
    out = pl.pallas_call(
        functools.partial(_mlp_kernel, tm=tm, is_ctx=is_ctx),
        grid=(rows // tm, D_FF // tf),
        in_specs=[pl.BlockSpec((tm, D_MODEL), lambda t, f: (t, 0)),
                  pl.BlockSpec((1, D_MODEL), lambda t, f: (0, 0)),
                  pl.BlockSpec((MOD_ROWS, D_MODEL), lambda t, f: (0, 3)),
                  pl.BlockSpec((MOD_ROWS, D_MODEL), lambda t, f: (0, 4)),
                  pl.BlockSpec((MOD_ROWS, D_MODEL), lambda t, f: (0, 5)),
                  pl.BlockSpec((D_MODEL, tf), lambda t, f: (0, f)),
                  pl.BlockSpec((tf, D_MODEL), lambda t, f: (f, 0))],
        out_specs=out_spec,
        out_shape=out_shape,
        scratch_shapes=[pltpu.VMEM((tm, D_MODEL), BF16), pltpu.VMEM((tm, D_MODEL), F32)] + (
            [] if is_ctx else [pltpu.VMEM((D_MODEL // LANES, tm, LANES), F32)]),
        compiler_params=_params("arbitrary", "arbitrary"),
        name="mlp_ctx" if is_ctx else "mlp",
    )(r_seg, norm_w.reshape(1, D_MODEL), mod_i, mod_i, mod_i, wu_bf16, wd_bf16)
    return out.reshape(rows, D_MODEL)


def _final_norm_kernel(x_ref, w_ref, o_ref):
    x = x_ref[...]
    o_ref[...] = x * lax.rsqrt(jnp.mean(x * x, axis=-1, keepdims=True) + EPS) * w_ref[...]


def final_norm(r, w):
    tm = 512
    return pl.pallas_call(
        _final_norm_kernel,
        grid=(N_LAT // tm,),
        in_specs=[pl.BlockSpec((tm, D_MODEL), lambda t: (t, 0)),
                  pl.BlockSpec((1, D_MODEL), lambda t: (0, 0))],
        out_specs=pl.BlockSpec((tm, D_MODEL), lambda t: (t, 0)),
        out_shape=jax.ShapeDtypeStruct((N_LAT, D_MODEL), F32),
        compiler_params=_params("arbitrary"),
        name="final_norm",
    )(r, w.reshape(1, D_MODEL))


LAT_TILES = SEQ // SEQ_TILE
N_SEQ_TILES = T_ROWS // SEQ_TILE
HALO_BLOCKS = SEQ_TILE // SUBLANES


def _seg_first(t):
    return jnp.logical_or(t >= BATCH * LAT_TILES, t % LAT_TILES == 0)


def _seg_last(t):
    return jnp.logical_or(t >= BATCH * LAT_TILES, t % LAT_TILES == LAT_TILES - 1)


def _conv4(ext_ref, cw_ref, tm, cols):
    ext = ext_ref[:, cols]
    rows = ext.shape[0]
    body = slice(SUBLANES, SUBLANES + tm)
    tap = lambda shift: ext[body] if shift == 0 else pltpu.roll(ext, shift % rows, 0)[body]
    acc = cw_ref[0:1, cols] * tap(2)
    acc = acc + cw_ref[1:2, cols] * tap(1)
    acc = acc + cw_ref[2:3, cols] * tap(0)
    acc = acc + cw_ref[3:4, cols] * tap(-1)
    return acc


def _fill_halo(ext_ref, prev_ref, cur_ref, next_ref, t, tm):
    zero = jnp.zeros((SUBLANES, ext_ref.shape[1]), F32)
    ext_ref[pl.ds(0, SUBLANES), :] = jnp.where(_seg_first(t), zero, prev_ref[...])
    ext_ref[pl.ds(SUBLANES, tm), :] = cur_ref[...]
    ext_ref[pl.ds(SUBLANES + tm, SUBLANES), :] = jnp.where(_seg_last(t), zero, next_ref[...])


def _halo_specs(width, col_of):
    last_blk = T_ROWS // SUBLANES - 1
    return [pl.BlockSpec((SUBLANES, width), lambda t, *g: (jnp.maximum(t * HALO_BLOCKS - 1, 0), col_of(t, *g))),
            pl.BlockSpec((SEQ_TILE, width), lambda t, *g: (t, col_of(t, *g))),
            pl.BlockSpec((SUBLANES, width),
                         lambda t, *g: (jnp.minimum((t + 1) * HALO_BLOCKS, last_blk), col_of(t, *g)))]


def _lru_prep_kernel(py_ref, xp_ref, xc_ref, xn_ref, cw_ref, cb_ref, wg_ref, bg_ref, lam_ref,
                     y_ref, a_ref, u_ref, ext_ref):
    t = pl.program_id(0)
    tm = SEQ_TILE
    y_ref[...] = jax.nn.gelu(py_ref[...])
    _fill_halo(ext_ref, xp_ref, xc_ref, xn_ref, t, tm)
    for n in range(LRU_BLOCKS):
        cols = slice(n * LRU_BW, (n + 1) * LRU_BW)
        xr = _conv4(ext_ref, cw_ref, tm, cols) + cb_ref[:, cols]
        xb = xr.astype(BF16)
        for d in range(2):
            gt = [jnp.dot(xb, wg_ref[d, g, n], preferred_element_type=F32) + bg_ref[d, g, :, cols] for g in range(2)]
            log_a = -LRU_C * jax.nn.sigmoid(gt[0]) * jax.nn.softplus(-lam_ref[d, :, cols])
            a = jnp.exp(log_a)
            a_ref[d, :, cols] = a
            one_minus_a2 = -jnp.tanh(log_a) * (a * a + 1.0)
            u_ref[d, :, cols] = jnp.sqrt(one_minus_a2) * jax.nn.sigmoid(gt[1]) * xr


def lru_prep(p, conv_w, conv_b, w_gate_bf16, b_gate, lam):
    w = LRU_W
    full = lambda shape: pl.BlockSpec(shape, lambda t: (0,) * len(shape))
    return pl.pallas_call(
        _lru_prep_kernel,
        grid=(N_SEQ_TILES,),
        in_specs=[pl.BlockSpec((SEQ_TILE, w), lambda t: (t, 0))] + _halo_specs(w, lambda t: 1) + [
            full((CONV_W, w)), full((1, w)),
            full((2, 2, LRU_BLOCKS, LRU_BW, LRU_BW)), full((2, 2, 1, w)), full((2, 1, w))],
        out_specs=[pl.BlockSpec((SEQ_TILE, w), lambda t: (t, 0)),
                   pl.BlockSpec((2, SEQ_TILE, w), lambda t: (0, t, 0)),
                   pl.BlockSpec((2, SEQ_TILE, w), lambda t: (0, t, 0))],
        out_shape=[jax.ShapeDtypeStruct((T_ROWS, w), F32),
                   jax.ShapeDtypeStruct((2, T_ROWS, w), F32),
                   jax.ShapeDtypeStruct((2, T_ROWS, w), F32)],
        scratch_shapes=[pltpu.VMEM((SEQ_TILE + 2 * SUBLANES, w), F32)],
        compiler_params=_params("arbitrary"),
        name="lru_prep",
    )(p, p, p, p, conv_w, conv_b.reshape(1, w), w_gate_bf16, b_gate.reshape(2, 2, 1, w), lam.reshape(2, 1, w))


def _lru_scan_kernel(a_ref, u_ref, o_ref, carry_ref, *, reverse):
    @pl.when(pl.program_id(2) == 0)
    def _():
        carry_ref[...] = jnp.zeros_like(carry_ref)

    nblk = SEQ_TILE // SUBLANES
    width = a_ref.shape[-1]
    row = lax.broadcasted_iota(jnp.int32, (SUBLANES, width), 0)

    def body(i, carry):
        blk = (nblk - 1 - i) if reverse else i
        r0 = pl.multiple_of(blk * SUBLANES, SUBLANES)
        a = a_ref[0, pl.ds(r0, SUBLANES), :]
        u = u_ref[0, pl.ds(r0, SUBLANES), :]
        for s in (1, 2, 4):
            if reverse:
                keep = row < SUBLANES - s
                shift = SUBLANES - s
            else:
                keep = row >= s
                shift = s
            a_sh = jnp.where(keep, pltpu.roll(a, shift, 0), 1.0)
            u_sh = jnp.where(keep, pltpu.roll(u, shift, 0), 0.0)
            u = a * u_sh + u
            a = a * a_sh
        h = u + a * carry
        o_ref[pl.ds(r0, SUBLANES), :] = h
        return h[0:1, :] if reverse else h[SUBLANES - 1:SUBLANES, :]

    carry_ref[...] = lax.fori_loop(0, nblk, body, carry_ref[...])


def lru_scan(a, u, direction):
    reverse = direction == 1
    tw = LRU_W

    def row_tile(b, k):
        lat = b * LAT_TILES + ((LAT_TILES - k) if reverse else (k - 1))
        return jnp.where(k == 0, BATCH * LAT_TILES + b, lat)

    spec3 = pl.BlockSpec((1, SEQ_TILE, tw), lambda b, l, k: (direction, row_tile(b, k), l))
    return pl.pallas_call(
        functools.partial(_lru_scan_kernel, reverse=reverse),
        grid=(BATCH, LRU_W // tw, LAT_TILES + 1),
        in_specs=[spec3, spec3],
        out_specs=pl.BlockSpec((SEQ_TILE, tw), lambda b, l, k: (row_tile(b, k), l)),
        out_shape=jax.ShapeDtypeStruct((T_ROWS, LRU_W), F32),
        scratch_shapes=[pltpu.VMEM((1, tw), F32)],
        compiler_params=_params("arbitrary", "arbitrary", "arbitrary"),
        name="lru_scan_bwd" if reverse else "lru_scan_fwd",
    )(a, u)


def _lru_post_kernel(hf_ref, hb_ref, y_ref, o_ref):
    o_ref[...] = ((hf_ref[...] + hb_ref[...]) * y_ref[...]).astype(BF16)


def lru_post(hf, hb, y):
    tm = 512
    spec = pl.BlockSpec((tm, LRU_W), lambda t: (t, 0))
    return pl.pallas_call(
        _lru_post_kernel,
        grid=(T_ROWS // tm,),
        in_specs=[spec, spec, spec],
        out_specs=spec,
        out_shape=jax.ShapeDtypeStruct((T_ROWS, LRU_W), BF16),
        compiler_params=_params("arbitrary"),
        name="lru_post",
    )(hf, hb, y)


def lru_mixer(p, conv_w, conv_b, w_gate, b_gate, lam):
    y, a, u = lru_prep(p, conv_w, conv_b, w_gate.astype(BF16), b_gate, lam)
    return lru_post(lru_scan(a, u, 0), lru_scan(a, u, 1), y)


PAIR_SLOTS = 16
NEG_BIG = -1e30


def _split3(x):
    hi = x.astype(BF16)
    r1 = x - hi.astype(F32)
    mid = r1.astype(BF16)
    lo = (r1 - mid.astype(F32)).astype(BF16)
    return hi, mid, lo


def _gates_kernel(x_ref, pa_ref, pb_ref, o_ref, orow_ref, *, kind, npairs):
    x = x_ref[...]
    tm = x.shape[0]
    lane = lax.broadcasted_iota(jnp.int32, x.shape, 1)
    if kind == "gdn":
        first = jax.nn.sigmoid(x)
        second = -jnp.exp(pa_ref[...]) * jax.nn.softplus(x + pb_ref[...])
    else:
        gt = GATE_CAP * jnp.tanh((x + pb_ref[...]) / GATE_CAP)
        first = gt
        second = jax.nn.log_sigmoid(gt)
    r = lax.broadcasted_iota(jnp.int32, (tm, tm), 0)
    s = lax.broadcasted_iota(jnp.int32, (tm, tm), 1)
    same = (r // CHUNK) == (s // CHUNK)
    as_w = lambda m: jnp.where(m, 1.0, 0.0).astype(BF16)
    parts = _split3(second)

    def chunk_sum(w):
        acc = jnp.dot(w, parts[0], preferred_element_type=F32)
        acc = acc + jnp.dot(w, parts[1], preferred_element_type=F32)
        return acc + jnp.dot(w, parts[2], preferred_element_type=F32)

    prefix = chunk_sum(as_w(jnp.logical_and(same, s <= r)))
    suffix = chunk_sum(as_w(jnp.logical_and(same, s >= r)))
    total = chunk_sum(as_w(same))
    cs = jnp.where((lane & 3) >= 2, suffix, prefix)
    x1 = jnp.where((lane & 7) < 4, first, cs)
    row_q = cs if kind == "gdn" else pltpu.roll(first, 4, 1) - cs
    row_t = row_q.T
    for j in range(npairs):
        o_ref[j] = jnp.concatenate([x1[:, 8 * j:8 * j + 8], total[:, 8 * j:8 * j + 8]], axis=1)
        orow_ref[j] = row_t[8 * j + 4:8 * j + 8, :]


def gate_columns(gates, row_a, row_b, kind, npairs):
    p, col_block = gates, 0
    return pl.pallas_call(
        functools.partial(_gates_kernel, kind=kind, npairs=npairs),
        grid=(N_SEQ_TILES,),
        in_specs=[pl.BlockSpec((SEQ_TILE, LANES), lambda t: (t, col_block)),
                  pl.BlockSpec((1, LANES), lambda t: (0, 0)),
                  pl.BlockSpec((1, LANES), lambda t: (0, 0))],
        out_specs=[pl.BlockSpec((npairs, SEQ_TILE, PAIR_SLOTS), lambda t: (0, t, 0)),
                   pl.BlockSpec((npairs, 4, SEQ_TILE), lambda t: (0, 0, t))],
        out_shape=[jax.ShapeDtypeStruct((npairs, T_ROWS, PAIR_SLOTS), F32),
                   jax.ShapeDtypeStruct((npairs, 4, T_ROWS), F32)],
        compiler_params=_params("arbitrary"),
        name=kind + "_gates",
    )(p, row_a, row_b)


STACK = 4 * CHUNK
CHUNKS_PER_TILE = SEQ_TILE // CHUNK


def _mm(a, b):
    return jnp.dot(a.astype(BF16), b.astype(BF16), preferred_element_type=F32)


def _mm_nt(a, b):
    return lax.dot_general(a.astype(BF16), b.astype(BF16), (((1,), (1,)), ((), ())),
                           preferred_element_type=F32)


def _stack_masks():
    r = lax.broadcasted_iota(jnp.int32, (STACK, STACK), 0)
    s = lax.broadcasted_iota(jnp.int32, (STACK, STACK), 1)
    same = (r // CHUNK) == (s // CHUNK)
    fwd = r < 2 * CHUNK
    ordered = jnp.logical_or(jnp.logical_and(fwd, s <= r), jnp.logical_and(jnp.logical_not(fwd), s >= r))
    incl = jnp.logical_and(same, ordered)
    strict = jnp.logical_and(incl, r != s)
    return r, s, incl, strict


def _col4(gf, gb, slot):
    return jnp.concatenate([gf[:, slot:slot + 1], gf[:, slot + 1:slot + 2],
                            gb[:, slot + 2:slot + 3], gb[:, slot + 3:slot + 4]], axis=0)


def _diag_blocks(x, rows0, width):
    return jnp.concatenate([x[rows0 + c * CHUNK:rows0 + (c + 1) * CHUNK, c * width:(c + 1) * width]
                            for c in range(4)], axis=0)


def _block_diag_place(v, r_col):
    chain = r_col // CHUNK
    return jnp.concatenate([jnp.where(chain == c, v, 0.0) for c in range(4)], axis=1)


def _chain_scale(col, width):
    blocks = []
    for c in range(4):
        t = jnp.broadcast_to(col[c * CHUNK:(c + 1) * CHUNK], (CHUNK, width))
        blocks.append(jnp.concatenate([t, t], axis=0))
    return jnp.concatenate(blocks, axis=1)


def _state_products(x, y, state, width):
    xs, ys = [], []
    for p in range(2):
        rows = slice(2 * p * CHUNK, (2 * p + 2) * CHUNK)
        out = _mm(jnp.concatenate([x[rows], y[rows]], axis=0), state[:, 2 * p * width:(2 * p + 2) * width])
        for e in range(2):
            cols = slice(e * width, (e + 1) * width)
            xs.append(out[e * CHUNK:(e + 1) * CHUNK, cols])
            ys.append(out[(2 + e) * CHUNK:(3 + e) * CHUNK, cols])
    return jnp.concatenate(xs, axis=0), jnp.concatenate(ys, axis=0)


def _scan_tile(b, k, reverse):
    lat = b * LAT_TILES + ((LAT_TILES - k) if reverse else (k - 1))
    return jnp.where(k == 0, BATCH * LAT_TILES + b, lat)


def _gdn_conv_kernel(xp_ref, xc_ref, xn_ref, cw_ref, o_ref, ext_ref):
    t = pl.program_id(0)
    ct = pl.program_id(1)
    _fill_halo(ext_ref, xp_ref, xc_ref, xn_ref, t, SEQ_TILE)
    qk_tiles = GDN_QK // GDN_CONV_TILE

    def conv_silu(h):
        cols = slice(h * GDN_DK, (h + 1) * GDN_DK)
        y = _conv4(ext_ref, cw_ref, SEQ_TILE, cols)
        return cols, y * jax.nn.sigmoid(y)

    @pl.when(ct >= 2 * qk_tiles)
    def _():
        for h in range(GDN_CONV_TILE // GDN_DK):
            cols, y = conv_silu(h)
            o_ref[:, cols] = y

    @pl.when(ct < 2 * qk_tiles)
    def _():
        scale = jnp.where(ct < qk_tiles, GDN_DK ** -0.5, 1.0)
        for h in range(GDN_CONV_TILE // GDN_DK):
            cols, y = conv_silu(h)
            nrm = y * lax.rsqrt(jnp.sum(y * y, axis=-1, keepdims=True) + EPS)
            o_ref[:, cols] = nrm * scale


GDN_CONV_TILE = 1024


def gdn_conv(p, conv_w):
    n = 2 * GDN_QK + GDN_VW
    tn = GDN_CONV_TILE
    return pl.pallas_call(
        _gdn_conv_kernel,
        grid=(N_SEQ_TILES, n // tn),
        in_specs=_halo_specs(tn, lambda t, c: c) + [pl.BlockSpec((CONV_W, tn), lambda t, c: (0, c))],
        out_specs=pl.BlockSpec((SEQ_TILE, tn), lambda t, c: (t, c)),
        out_shape=jax.ShapeDtypeStruct((T_ROWS, n), F32),
        scratch_shapes=[pltpu.VMEM((SEQ_TILE + 2 * SUBLANES, tn), F32)],
        compiler_params=_params("arbitrary", "arbitrary"),
        name="gdn_conv",
    )(p, p, p, conv_w)


def _row4(rf_ref, rb_ref, hp, rf, rb):
    return jnp.concatenate([rf_ref[hp, 0:1, rf:rf + CHUNK], rf_ref[hp, 1:2, rf:rf + CHUNK],
                            rb_ref[hp, 2:3, rb:rb + CHUNK], rb_ref[hp, 3:4, rb:rb + CHUNK]], axis=1)


GDN_STEP_PAIRS = 2
GDN_GROUP_CHUNKS = 1
GDN_STAGE_LAG = 2


def _gdn_core_kernel(qf_ref, kf_ref, vf_ref, gf_ref, grf_ref, qb_ref, kb_ref, vb_ref, gb_ref, grb_ref,
                     of_ref, ob_ref, s_ref):
    @pl.when(pl.program_id(2) == 0)
    def _():
        s_ref[...] = jnp.zeros_like(s_ref)

    r, s, incl, strict = _stack_masks()
    eye = r == s
    blk8 = (r // 8) == (s // 8)
    blk16 = (r // 16) == (s // 16)
    blk32 = (r // 32) == (s // 32)
    r_col = lax.broadcasted_iota(jnp.int32, (STACK, 1), 0)

    pairs = range(GDN_STEP_PAIRS)
    steps = range(CHUNKS_PER_TILE)
    rows_f = [n * CHUNK for n in steps]
    rows_b = [(CHUNKS_PER_TILE - 1 - n) * CHUNK for n in steps]
    problems = [(hp, n) for n in steps for hp in pairs]
    each = lambda fn, *lists: [fn(*xs) for xs in zip(*lists)]

    def load(hp, n):
        rf, rb = rows_f[n], rows_b[n]
        dk = slice(hp * GDN_DK, (hp + 1) * GDN_DK)
        v_of = lambda e: slice((2 * hp + e) * GDN_DV, (2 * hp + e + 1) * GDN_DV)
        kf = kf_ref[pl.ds(rf, CHUNK), dk]
        kb = kb_ref[pl.ds(rb, CHUNK), dk]
        qf = qf_ref[pl.ds(rf, CHUNK), dk]
        qb = qb_ref[pl.ds(rb, CHUNK), dk]
        k4 = jnp.concatenate([kf, kf, kb, kb], axis=0)
        q4 = jnp.concatenate([qf, qf, qb, qb], axis=0)
        v4 = jnp.concatenate([vf_ref[pl.ds(rf, CHUNK), v_of(0)], vf_ref[pl.ds(rf, CHUNK), v_of(1)],
                              vb_ref[pl.ds(rb, CHUNK), v_of(0)], vb_ref[pl.ds(rb, CHUNK), v_of(1)]], axis=0)
        gf = gf_ref[hp, pl.ds(rf, CHUNK), :]
        gb = gb_ref[hp, pl.ds(rb, CHUNK), :]
        return (k4, q4, v4, _col4(gf, gb, 0), _col4(gf, gb, 4), _col4(gf, gb, 12),
                _row4(grf_ref, grb_ref, hp, rf, rb))

    def gram_of(k, q):
        f, b = slice(0, CHUNK), slice(2 * CHUNK, 3 * CHUNK)
        g = _mm_nt(jnp.concatenate([k[f], k[b], q[f], q[b]], axis=0), k)
        return jnp.concatenate([g[i * CHUNK:(i + 1) * CHUNK] for i in (0, 0, 1, 1, 2, 2, 3, 3)], axis=0)

    def merge_fn(blk, level_mask):
        half = blk // 2
        starts = range(0, STACK, blk)
        active = [s0 + (half if s0 < 2 * CHUNK else 0) for s0 in starts]

        def left(a, x):
            rows = jnp.concatenate([a[s0:s0 + half] for s0 in active], axis=0)
            return _mm(rows, jnp.where(level_mask, x, 0.0))

        def right(a, y):
            z = _mm(y, a)
            out = []
            for i, s0 in enumerate(starts):
                upd = a[active[i]:active[i] + half] - z[i * half:(i + 1) * half]
                keep = a[s0:s0 + half] if active[i] != s0 else a[s0 + half:s0 + blk]
                out += [keep, upd] if active[i] != s0 else [upd, keep]
            return jnp.concatenate(out, axis=0)

        return left, right

    merge16 = merge_fn(16, jnp.logical_and(blk16, jnp.logical_not(blk8)))
    merge32 = merge_fn(32, jnp.logical_and(blk32, jnp.logical_not(blk16)))
    merge64 = merge_fn(CHUNK, jnp.logical_not(blk32))

    def phase_a(chunks):
        idx = [(hp, n) for n in chunks for hp in pairs]
        v = {}

        def s_load():
            v["k4"], v["q4"], v["v4"], v["beta"], v["gc"], v["gl"], v["gc_row"] = zip(*[load(hp, n) for hp, n in idx])
            v["eg"] = each(jnp.exp, v["gc"])
            v["gram"] = each(gram_of, v["k4"], v["q4"])

        def s_nmat():
            decay = each(lambda g, g_row: jnp.exp(jnp.where(incl, g - g_row, NEG_BIG)), v["gc"], v["gc_row"])
            v["nmat"] = each(lambda gm, d, b: jnp.where(strict, gm[:STACK] * d * b, 0.0),
                             v["gram"], decay, v["beta"])
            v["a_qk"] = each(lambda gm, d: gm[STACK:] * d, v["gram"], decay)
            v["n0"] = each(lambda x: jnp.where(blk8, x, 0.0), v["nmat"])
            v["t"] = each(lambda x: jnp.where(eye, 1.0, 0.0) - x, v["n0"])

        def s_p2():
            v["p2"] = each(_mm, v["n0"], v["n0"])

        def s_p4():
            v["p4"] = each(_mm, v["p2"], v["p2"])
            v["t"] = each(lambda a, b: a + _mm(a, b), v["t"], v["p2"])

        def s_t8():
            v["t"] = each(lambda a, b: a + _mm(a, b), v["t"], v["p4"])

        def s_left(merge):
            return lambda: v.__setitem__("y", each(merge[0], v["t"], v["nmat"]))

        def s_right(merge):
            return lambda: v.__setitem__("t", each(merge[1], v["t"], v["y"]))

        def s_uw():
            v["uw"] = each(lambda a, x, k, b, e: _mm(a, jnp.concatenate([x * b, k * (b * e)], axis=1)),
                           v["t"], v["v4"], v["k4"], v["beta"], v["eg"])
            v["qe"] = each(lambda q, e: q * e, v["q4"], v["eg"])
            v["k_t"] = each(lambda k, g, gt: (k * jnp.exp(gt - g)).T, v["k4"], v["gc"], v["gl"])
            v["s_scale"] = each(lambda gt: _chain_scale(jnp.exp(gt), GDN_DV), v["gl"])

        stages = [s_load, s_nmat, s_p2, s_p4, s_t8, s_left(merge16), s_right(merge16), s_left(merge32),
                  s_right(merge32), s_left(merge64), s_right(merge64), s_uw]
        return v, stages, {n: i * GDN_STEP_PAIRS for i, n in enumerate(chunks)}

    s_all = [s_ref[hp] for hp in pairs]

    def phase_b(group, n):
        v, _, base = group
        tmp = {}

        def s_state():
            for hp in pairs:
                i = base[n] + hp
                ws, qs = _state_products(v["uw"][i][:, GDN_DV:], v["qe"][i], s_all[hp], GDN_DV)
                tmp[hp] = (v["uw"][i][:, :GDN_DV] - ws, qs)

        def s_update():
            for hp in pairs:
                i = base[n] + hp
                v_new, qs = tmp[hp]
                o4 = qs + _mm(v["a_qk"][i], v_new)
                s_all[hp] = s_all[hp] * v["s_scale"][i] + _mm(v["k_t"][i], _block_diag_place(v_new, r_col))
                rf, rb = rows_f[n], rows_b[n]
                for c, (o_ref, r0) in enumerate(((of_ref, rf), (of_ref, rf), (ob_ref, rb), (ob_ref, rb))):
                    col = (2 * hp + c % 2) * GDN_DV
                    o_ref[pl.ds(r0, CHUNK), col:col + GDN_DV] = o4[c * CHUNK:(c + 1) * CHUNK]

        return [s_state, s_update]

    slots = {}
    for g in range(0, CHUNKS_PER_TILE, GDN_GROUP_CHUNKS):
        chunks = list(steps)[g:g + GDN_GROUP_CHUNKS]
        group = phase_a(chunks)
        sequence = group[1] + [st for n in chunks for st in phase_b(group, n)]
        for k, st in enumerate(sequence):
            slots.setdefault((g // GDN_GROUP_CHUNKS) * GDN_STAGE_LAG + k, []).append(st)
    for slot in sorted(slots):
        for st in slots[slot]:
            st()
    for hp in pairs:
        s_ref[hp] = s_all[hp]


def gdn_core(qkv, gcols, grows):
    hp = GDN_STEP_PAIRS
    kcol = GDN_QK // (hp * GDN_DK)
    vcol = 2 * GDN_QK // (hp * 2 * GDN_DV)

    def specs(reverse):
        tile = lambda b, j, k: _scan_tile(b, k, reverse)
        return [pl.BlockSpec((SEQ_TILE, hp * GDN_DK), lambda b, j, k: (tile(b, j, k), j)),
                pl.BlockSpec((SEQ_TILE, hp * GDN_DK), lambda b, j, k: (tile(b, j, k), kcol + j)),
                pl.BlockSpec((SEQ_TILE, hp * 2 * GDN_DV), lambda b, j, k: (tile(b, j, k), vcol + j)),
                pl.BlockSpec((hp, SEQ_TILE, PAIR_SLOTS), lambda b, j, k: (j, tile(b, j, k), 0)),
                pl.BlockSpec((hp, 4, SEQ_TILE), lambda b, j, k: (j, 0, tile(b, j, k)))]

    out_spec = lambda reverse: pl.BlockSpec((SEQ_TILE, hp * 2 * GDN_DV),
                                            lambda b, j, k: (_scan_tile(b, k, reverse), j))
    return pl.pallas_call(
        _gdn_core_kernel,
        grid=(BATCH, GDN_QK_HEADS // hp, LAT_TILES + 1),
        in_specs=specs(False) + specs(True),
        out_specs=[out_spec(False), out_spec(True)],
        out_shape=[jax.ShapeDtypeStruct((T_ROWS, GDN_VW), F32)] * 2,
        scratch_shapes=[pltpu.VMEM((hp, GDN_DK, 4 * GDN_DV), F32)],
        compiler_params=_params("arbitrary", "arbitrary", "arbitrary"),
        name="gdn_core",
    )(qkv, qkv, qkv, gcols, grows, qkv, qkv, qkv, gcols, grows)


def _head_norm_gate_kernel(of_ref, ob_ref, z_ref, nw_ref, y_ref, *, head_dim, gate):
    for h in range(of_ref.shape[1] // head_dim):
        sl = slice(h * head_dim, (h + 1) * head_dim)
        o = of_ref[:, sl] + ob_ref[:, sl]
        y = o * lax.rsqrt(jnp.mean(o * o, axis=-1, keepdims=True) + EPS) * nw_ref[:, sl]
        z = z_ref[:, sl]
        g = z * jax.nn.sigmoid(z) if gate == "silu" else jax.nn.sigmoid(z)
        y_ref[:, sl] = (y * g).astype(BF16)


def head_norm_gate(o_f, o_b, p, z_col_block, norm_row, head_dim, gate):
    width = o_f.shape[1]
    tm, tn = 512, 512
    spec = pl.BlockSpec((tm, tn), lambda t, c: (t, c))
    return pl.pallas_call(
        functools.partial(_head_norm_gate_kernel, head_dim=head_dim, gate=gate),
        grid=(T_ROWS // tm, width // tn),
        in_specs=[spec, spec, pl.BlockSpec((tm, tn), lambda t, c: (t, z_col_block + c)),
                  pl.BlockSpec((1, tn), lambda t, c: (0, c))],
        out_specs=spec,
        out_shape=jax.ShapeDtypeStruct((T_ROWS, width), BF16),
        compiler_params=_params("arbitrary", "arbitrary"),
        name="head_norm_" + gate,
    )(o_f, o_b, p, norm_row)


def gdn_mixer(p, gates, conv_w, a_log, dt_bias, norm_w):
    zeros4 = jnp.zeros((GDN_QK_HEADS, 4), F32)
    per_pair = lambda t: t.reshape(2, GDN_QK_HEADS, 2).transpose(1, 0, 2).reshape(GDN_QK_HEADS, 4)
    row = lambda t: jnp.concatenate([zeros4, per_pair(t)], axis=1).reshape(1, LANES)
    gcols, grows = gate_columns(gates, row(a_log), row(dt_bias), "gdn", GDN_QK_HEADS)
    qkv = gdn_conv(p, conv_w)
    o_f, o_b = gdn_core(qkv, gcols, grows)
    norm_row = jnp.tile(norm_w, GDN_V_HEADS).reshape(1, GDN_VW)
    return head_norm_gate(o_f, o_b, p, (2 * GDN_QK + GDN_VW) // 512, norm_row, GDN_DV, "silu")


def gdn_in_weights(w_in):
    base = 2 * GDN_QK + 2 * GDN_VW
    gates = w_in[:, base:].reshape(D_MODEL, 2, 2, GDN_QK_HEADS, 2)
    gates = gates.transpose(0, 3, 1, 2, 4).reshape(D_MODEL, 4 * GDN_V_HEADS)
    return w_in[:, :base].astype(BF16), gates.astype(BF16)


ML_PAIRS = ML_HEADS // 2


def _mlstm_core_kernel(qf_ref, kf_ref, vf_ref, gf_ref, grf_ref, qb_ref, kb_ref, vb_ref, gb_ref, grb_ref,
                       of_ref, ob_ref, c_ref, n_ref, m_ref):
    @pl.when(pl.program_id(2) == 0)
    def _():
        c_ref[...] = jnp.zeros_like(c_ref)
        n_ref[...] = jnp.zeros_like(n_ref)
        m_ref[...] = jnp.zeros_like(m_ref)

    r, s, incl, _ = _stack_masks()
    r_col = lax.broadcasted_iota(jnp.int32, (STACK, 1), 0)
    last_rows = (CHUNK - 1, 2 * CHUNK - 1, 2 * CHUNK, 3 * CHUNK)

    steps = range(CHUNKS_PER_TILE)
    rows_f = [n * CHUNK for n in steps]
    rows_b = [(CHUNKS_PER_TILE - 1 - n) * CHUNK for n in steps]
    each = lambda fn, *lists: [fn(*xs) for xs in zip(*lists)]

    def load(rf, rb):
        def stack(f_ref, b_ref, w):
            return jnp.concatenate([f_ref[pl.ds(rf, CHUNK), 0:w], f_ref[pl.ds(rf, CHUNK), w:2 * w],
                                    b_ref[pl.ds(rb, CHUNK), 0:w], b_ref[pl.ds(rb, CHUNK), w:2 * w]], axis=0)

        gf = gf_ref[0, pl.ds(rf, CHUNK), :]
        gb = gb_ref[0, pl.ds(rb, CHUNK), :]
        return (stack(qf_ref, qb_ref, ML_DQK) * ML_DQK ** -0.5, stack(kf_ref, kb_ref, ML_DQK),
                stack(vf_ref, vb_ref, ML_DV), _col4(gf, gb, 0), _col4(gf, gb, 4), _col4(gf, gb, 12),
                _row4(grf_ref, grb_ref, 0, rf, rb))

    q4, k4, v4, ig, bc, bl, ib_row = zip(*each(load, rows_f, rows_b))
    d_log = each(lambda b, row: jnp.where(incl, b + row, NEG_BIG), bc, ib_row)
    row_max = each(lambda d: jnp.max(d, axis=1, keepdims=True), d_log)
    w_max = each(lambda rm: jnp.concatenate([jnp.broadcast_to(rm[i:i + 1], (CHUNK, 1)) for i in last_rows], axis=0),
                 row_max)
    qk = each(_mm_nt, q4, k4)

    m_in = [m_ref[...]]
    for n in steps:
        m_in.append(jnp.maximum(bl[n] + m_in[n], w_max[n]))
    m_out = m_in[1:]
    m_ref[...] = m_in[-1]

    inter = each(lambda b, m: b + m, bc, m_in)
    m_t = each(jnp.maximum, inter, row_max)
    smat = each(lambda x, d, mt: x * jnp.exp(d - mt), qk, d_log, m_t)
    w_inter = each(lambda i, mt: jnp.exp(i - mt), inter, m_t)
    intra = each(_mm, smat, v4)
    s_sum = each(lambda x: jnp.sum(x, axis=1, keepdims=True), smat)
    cd = each(lambda b, m0, m1: jnp.exp(b + m0 - m1), bl, m_in, m_out)
    kw = each(lambda k, b, c, i, m1: k * jnp.exp(b - c + i - m1), k4, bl, bc, ig, m_out)
    d_c = each(lambda x, v: _mm(x.T, _block_diag_place(v, r_col)), kw, v4)
    c_scale = each(lambda x: _chain_scale(x, ML_DV), cd)

    c_all = c_ref[...]
    n_all = [n_ref[c:c + 1, :] for c in range(4)]
    for n in steps:
        qc = []
        for p in range(2):
            out = _mm(q4[n][2 * p * CHUNK:(2 * p + 2) * CHUNK], c_all[:, 2 * p * ML_DV:(2 * p + 2) * ML_DV])
            qc += [out[e * CHUNK:(e + 1) * CHUNK, e * ML_DV:(e + 1) * ML_DV] for e in range(2)]
        qc = jnp.concatenate(qc, axis=0)
        num = w_inter[n] * qc + intra[n]
        n_rows = jnp.concatenate([jnp.broadcast_to(x, (CHUNK, ML_DQK)) for x in n_all], axis=0)
        den = w_inter[n] * jnp.sum(q4[n] * n_rows, axis=1, keepdims=True) + s_sum[n]
        h4 = num / jnp.maximum(jnp.abs(den), jnp.exp(-m_t[n]))
        c_all = c_all * c_scale[n] + d_c[n]
        n_all = [cd[n][c * CHUNK:c * CHUNK + 1] * n_all[c]
                 + jnp.sum(kw[n][c * CHUNK:(c + 1) * CHUNK], axis=0, keepdims=True) for c in range(4)]
        rf, rb = rows_f[n], rows_b[n]
        of_ref[pl.ds(rf, CHUNK), 0:ML_DV] = h4[0:CHUNK]
        of_ref[pl.ds(rf, CHUNK), ML_DV:2 * ML_DV] = h4[CHUNK:2 * CHUNK]
        ob_ref[pl.ds(rb, CHUNK), 0:ML_DV] = h4[2 * CHUNK:3 * CHUNK]
        ob_ref[pl.ds(rb, CHUNK), ML_DV:2 * ML_DV] = h4[3 * CHUNK:4 * CHUNK]
    c_ref[...] = c_all
    for c in range(4):
        n_ref[c:c + 1, :] = n_all[c]


def mlstm_core(p, gcols, grows):
    kcol = ML_QK // (2 * ML_DQK)
    vcol = 2 * ML_QK // (2 * ML_DV)

    def specs(reverse):
        tile = lambda b, j, k: _scan_tile(b, k, reverse)
        return [pl.BlockSpec((SEQ_TILE, 2 * ML_DQK), lambda b, j, k: (tile(b, j, k), j)),
                pl.BlockSpec((SEQ_TILE, 2 * ML_DQK), lambda b, j, k: (tile(b, j, k), kcol + j)),
                pl.BlockSpec((SEQ_TILE, 2 * ML_DV), lambda b, j, k: (tile(b, j, k), vcol + j)),
                pl.BlockSpec((1, SEQ_TILE, PAIR_SLOTS), lambda b, j, k: (j, tile(b, j, k), 0)),
                pl.BlockSpec((1, 4, SEQ_TILE), lambda b, j, k: (j, 0, tile(b, j, k)))]

    out_spec = lambda reverse: pl.BlockSpec((SEQ_TILE, 2 * ML_DV),
                                            lambda b, j, k: (_scan_tile(b, k, reverse), j))
    return pl.pallas_call(
        _mlstm_core_kernel,
        grid=(BATCH, ML_PAIRS, LAT_TILES + 1),
        in_specs=specs(False) + specs(True),
        out_specs=[out_spec(False), out_spec(True)],
        out_shape=[jax.ShapeDtypeStruct((T_ROWS, ML_V), F32)] * 2,
        scratch_shapes=[pltpu.VMEM((ML_DQK, 4 * ML_DV), F32), pltpu.VMEM((SUBLANES, ML_DQK), F32),
                        pltpu.VMEM((STACK, 1), F32)],
        compiler_params=_params("arbitrary", "arbitrary", "arbitrary"),
        name="mlstm_core",
    )(p, p, p, gcols, grows, p, p, p, gcols, grows)


def mlstm_mixer(p, gates, gate_b, norm_w):
    bias = gate_b.reshape(2, 2, ML_PAIRS, 2).transpose(2, 1, 0, 3).reshape(1, 4 * ML_HEADS)
    bias = jnp.pad(bias, ((0, 0), (0, LANES - 4 * ML_HEADS)))
    gcols, grows = gate_columns(gates, bias, bias, "mlstm", ML_PAIRS)
    h_f, h_b = mlstm_core(p, gcols, grows)
    return head_norm_gate(h_f, h_b, p, (2 * ML_QK + ML_V) // 512, norm_w.reshape(1, ML_V), ML_DV, "sigmoid")


def mlstm_in_weights(w_in):
    base = 2 * ML_QK + 2 * ML_V
    gates = w_in[:, base:].reshape(D_MODEL, 2, 2, ML_PAIRS, 2)
    gates = gates.transpose(0, 3, 2, 1, 4).reshape(D_MODEL, 4 * ML_HEADS)
    return w_in[:, :base].astype(BF16), _pad_cols(gates, LANES).astype(BF16)


def _pad_cols(w, mult):
    n = w.shape[1]
    return jnp.pad(w, ((0, 0), (0, (-n) % mult)))


def kernel(x, c, ctx, c_ctx, w_mod, b_mod, norm_mix, norm_ff, norm_out, gdn_w_in, gdn_conv, gdn_a_log, gdn_dt_bias, gdn_norm, gdn_w_out, ml_w_in, ml_gate_b, ml_norm, ml_w_out, lru_w_in, lru_conv, lru_conv_b, lru_w_gate, lru_b_gate, lru_lambda, lru_w_out, ff_w_up, ff_w_down):
    assert DEPTH % 2 == 0
    r_lat = x.reshape(N_LAT, D_MODEL)
    r_ctx = ctx.reshape(N_CTX, D_MODEL)
    c_all = jnp.concatenate([c, c_ctx[None], jnp.zeros((MOD_ROWS - BATCH - 1, D_MODEL), F32)], axis=0)
    mod = modulation_table(c_all, w_mod, b_mod)
    for i in range(DEPTH):
        kind, j = i % N_MIXERS, i // N_MIXERS
        if kind == 0:
            p, gates = in_proj(r_lat, r_ctx, norm_mix[i], mod[i], *gdn_in_weights(gdn_w_in[j]))
            y = gdn_mixer(p, gates, gdn_conv[j], gdn_a_log[j], gdn_dt_bias[j], gdn_norm[j])
            w_out = gdn_w_out[j]
        elif kind == 1:
            p, gates = in_proj(r_lat, r_ctx, norm_mix[i], mod[i], *mlstm_in_weights(ml_w_in[j]))
            y = mlstm_mixer(p, gates, ml_gate_b[j], ml_norm[j])
            w_out = ml_w_out[j]
        else:
            no_gates = jnp.zeros((D_MODEL, LANES), BF16)
            p, _ = in_proj(r_lat, r_ctx, norm_mix[i], mod[i], lru_w_in[j].astype(BF16), no_gates)
            y = lru_mixer(p, lru_conv[j], lru_conv_b[j], lru_w_gate[j], lru_b_gate[j], lru_lambda[j])
            w_out = lru_w_out[j]
        w_out, w_up, w_down = w_out.astype(BF16), ff_w_up[i].astype(BF16), ff_w_down[i].astype(BF16)
        r_lat = out_proj(y, w_out, r_lat, mod[i], is_ctx=False)
        r_lat = mlp(r_lat, norm_ff[i], mod[i], w_up, w_down, is_ctx=False)
        if i < DEPTH - 1:
            r_ctx = out_proj(y, w_out, r_ctx, mod[i], is_ctx=True)
            r_ctx = mlp(r_ctx, norm_ff[i], mod[i], w_up, w_down, is_ctx=True)
    return final_norm(r_lat, norm_out).reshape(BATCH, SEQ, D_MODEL)
```

```python
import functools
import math

import jax
import jax.numpy as jnp
from jax import lax
from jax.experimental import pallas as pl
from jax.experimental.pallas import tpu as pltpu

F32 = jnp.float32
BF16 = jnp.bfloat16

D_MODEL = 2048
BATCH = 4
SEQ = 4096
DEPTH = 4
CTX_LEN = 256
GRID_W = 64
N_MIXERS = 3
CHUNK = 64
CONV_W = 4
EPS = 1e-6
D_FF = 4 * D_MODEL

GDN_QK_HEADS = D_MODEL // 128
GDN_V_HEADS = 2 * GDN_QK_HEADS
GDN_DK = 128
GDN_DV = 128
GDN_QK = GDN_QK_HEADS * GDN_DK
GDN_VW = GDN_V_HEADS * GDN_DV
GDN_REP = GDN_V_HEADS // GDN_QK_HEADS
GDN_IN = 2 * GDN_QK + 2 * GDN_VW + 4 * GDN_V_HEADS

ML_HEADS = 8
ML_DQK = D_MODEL // (2 * ML_HEADS)
ML_DV = D_MODEL // ML_HEADS
ML_QK = ML_HEADS * ML_DQK
ML_V = ML_HEADS * ML_DV
ML_IN = 2 * ML_QK + 2 * ML_V + 4 * ML_HEADS
GATE_CAP = 15.0

LRU_W = D_MODEL
LRU_BLOCKS = 8
LRU_BW = LRU_W // LRU_BLOCKS
LRU_C = 8.0

N_LAT = BATCH * SEQ
N_CTX = BATCH * CTX_LEN
T_ROWS = N_LAT + N_CTX
MOD_ROWS = 8
CTX_MOD_ROW = BATCH

V7X_VMEM_BYTES = 64 * 1024 * 1024
VMEM_LIMIT = 56 * 1024 * 1024
SUBLANES = 8
LANES = 128

ROW_TILE = 1024
MLP_ROW_TILE = 512
COL_TILE = 512
IN_PROJ_MAX_COL_TILE = 1280
SEQ_TILE = 256


def _params(*sem):
    return pltpu.CompilerParams(dimension_semantics=sem, vmem_limit_bytes=VMEM_LIMIT)


def _mod_row(tile, tile_rows):
    lat_tiles = N_LAT // tile_rows
    return jnp.where(tile < lat_tiles, tile // (SEQ // tile_rows), CTX_MOD_ROW)


def _mod_kernel(c_ref, w_ref, b_ref, o_ref):
    c = c_ref[...]
    s = c * jax.nn.sigmoid(c)
    o_ref[0] = jnp.dot(s.astype(BF16), w_ref[0].astype(BF16), preferred_element_type=F32) + b_ref[0]


def modulation_table(c_all, w_mod, b_mod):
    tn = 1024
    n6 = 6 * D_MODEL
    return pl.pallas_call(
        _mod_kernel,
        grid=(DEPTH, n6 // tn),
        in_specs=[pl.BlockSpec((MOD_ROWS, D_MODEL), lambda i, n: (0, 0)),
                  pl.BlockSpec((1, D_MODEL, tn), lambda i, n: (i, 0, n)),
                  pl.BlockSpec((1, 1, tn), lambda i, n: (i, 0, n))],
        out_specs=pl.BlockSpec((1, MOD_ROWS, tn), lambda i, n: (i, 0, n)),
        out_shape=jax.ShapeDtypeStruct((DEPTH, MOD_ROWS, n6), F32),
        compiler_params=_params("arbitrary", "arbitrary"),
        name="modulation_table",
    )(c_all, w_mod, b_mod.reshape(DEPTH, 1, n6))


def _norm_modulate(x, nw, shift, scale):
    y = x * lax.rsqrt(jnp.mean(x * x, axis=-1, keepdims=True) + EPS)
    return (y * nw) * (1.0 + scale) + shift


def _in_proj_kernel(xl_ref, xc_ref, nw_ref, sh_ref, sc_ref, w_ref, wg_ref, o_ref, og_ref, h_ref, *, tm):
    t = pl.program_id(0)
    row = _mod_row(t, tm)
    lat_tiles = N_LAT // tm

    def prologue(x_ref):
        h = _norm_modulate(x_ref[...], nw_ref[...], sh_ref[pl.ds(row, 1), :], sc_ref[pl.ds(row, 1), :])
        h_ref[...] = h.astype(BF16)
        og_ref[...] = jnp.dot(h_ref[...], wg_ref[...], preferred_element_type=F32)

    @pl.when(jnp.logical_and(pl.program_id(1) == 0, t < lat_tiles))
    def _():
        prologue(xl_ref)

    @pl.when(jnp.logical_and(pl.program_id(1) == 0, t >= lat_tiles))
    def _():
        prologue(xc_ref)

    o_ref[...] = jnp.dot(h_ref[...], w_ref[...], preferred_element_type=F32)


def in_proj(r_lat, r_ctx, norm_w, mod_i, w_bf16, w_gates_bf16, layer=None, n=None):
    n = w_bf16.shape[-1] if n is None else n
    w_spec = (lambda tn: pl.BlockSpec((D_MODEL, tn), lambda t, j: (0, j))) if layer is None else (
        lambda tn: pl.BlockSpec((None, D_MODEL, tn), lambda t, j: (layer, 0, j)))
    tm = ROW_TILE
    tn = max(d * LANES for d in range(1, IN_PROJ_MAX_COL_TILE // LANES + 1) if (n // LANES) % d == 0)
    lat_tiles = N_LAT // tm
    assert N_CTX == tm and n % LANES == 0
    once = pl.Buffered(1)
    return pl.pallas_call(
        functools.partial(_in_proj_kernel, tm=tm),
        grid=(T_ROWS // tm, n // tn),
        in_specs=[pl.BlockSpec((tm, D_MODEL), lambda t, j: (jnp.minimum(t, lat_tiles - 1), 0), pipeline_mode=once),
                  pl.BlockSpec((tm, D_MODEL), lambda t, j: (0, 0), pipeline_mode=once),
                  pl.BlockSpec((1, D_MODEL), lambda t, j: (0, 0)),
                  pl.BlockSpec((MOD_ROWS, D_MODEL), lambda t, j: (0, 0)),
                  pl.BlockSpec((MOD_ROWS, D_MODEL), lambda t, j: (0, 1)),
                  w_spec(tn),
                  pl.BlockSpec((D_MODEL, LANES), lambda t, j: (0, 0))],
        out_specs=[pl.BlockSpec((tm, tn), lambda t, j: (t, j)),
                   pl.BlockSpec((tm, LANES), lambda t, j: (t, 0))],
        out_shape=[jax.ShapeDtypeStruct((T_ROWS, n), F32), jax.ShapeDtypeStruct((T_ROWS, LANES), F32)],
        scratch_shapes=[pltpu.VMEM((tm, D_MODEL), BF16)],
        compiler_params=_params("arbitrary", "arbitrary"),
        name="in_proj",
    )(r_lat, r_ctx, norm_w.reshape(1, D_MODEL), mod_i, mod_i, w_bf16, w_gates_bf16)


def _segment_mod_row(tile, tile_rows, is_ctx):
    return CTX_MOD_ROW if is_ctx else tile // (SEQ // tile_rows)


def _out_proj_kernel(y_ref, w_ref, r_ref, g_ref, o_ref, *, tm, is_ctx):
    row = _segment_mod_row(pl.program_id(0), tm, is_ctx)
    acc = jnp.dot(y_ref[...], w_ref[...], preferred_element_type=F32)
    o_ref[...] = r_ref[...] + g_ref[pl.ds(row, 1), :] * acc


def out_proj(y_bf16, w_bf16, r_seg, mod_i, is_ctx):
    k = y_bf16.shape[1]
    tm, tn = ROW_TILE, COL_TILE
    nj = D_MODEL // tn
    rows = r_seg.shape[0]
    tile0 = N_LAT // tm if is_ctx else 0
    return pl.pallas_call(
        functools.partial(_out_proj_kernel, tm=tm, is_ctx=is_ctx),
        grid=(rows // tm, nj),
        in_specs=[pl.BlockSpec((tm, k), lambda t, j: (tile0 + t, 0)),
                  pl.BlockSpec((k, tn), lambda t, j: (0, j)),
                  pl.BlockSpec((tm, tn), lambda t, j: (t, j)),
                  pl.BlockSpec((MOD_ROWS, tn), lambda t, j: (0, 2 * nj + j))],
        out_specs=pl.BlockSpec((tm, tn), lambda t, j: (t, j)),
        out_shape=jax.ShapeDtypeStruct((rows, D_MODEL), F32),
        compiler_params=_params("arbitrary", "arbitrary"),
        name="out_proj_ctx" if is_ctx else "out_proj",
    )(y_bf16, w_bf16, r_seg, mod_i)


def _mlp_kernel(x_ref, nw_ref, sh_ref, sc_ref, g_ref, wu_ref, wd_ref, o_ref, h_ref, acc_ref, *maybe_tr_ref,
                tm, is_ctx):
    tr_ref = None if is_ctx else maybe_tr_ref[0]
    f = pl.program_id(1)
    row = _segment_mod_row(pl.program_id(0), tm, is_ctx)

    @pl.when(f == 0)
    def _():
        h = _norm_modulate(x_ref[...], nw_ref[...], sh_ref[pl.ds(row, 1), :], sc_ref[pl.ds(row, 1), :])
        h_ref[...] = h.astype(BF16)
        acc_ref[...] = jnp.zeros_like(acc_ref)

    u = jnp.dot(h_ref[...], wu_ref[...], preferred_element_type=F32)
    a = jnp.square(jnp.maximum(u, 0.0))
    acc_ref[...] += jnp.dot(a.astype(BF16), wd_ref[...], preferred_element_type=F32)

    @pl.when(f == pl.num_programs(1) - 1)
    def _():
        res = x_ref[...] + g_ref[pl.ds(row, 1), :] * acc_ref[...]
        if is_ctx:
            o_ref[...] = res
        else:
            lane_tiles = D_MODEL // LANES
            for c in range(lane_tiles):
                tr_ref[c] = res[:, c * LANES:(c + 1) * LANES]
            for b in range(GRID_W):
                for c in range(lane_tiles):
                    o_ref[b, :, c * LANES:(c + 1) * LANES] = tr_ref.at[c][pl.ds(b, tm // GRID_W, stride=GRID_W), :]


def mlp(r_seg, norm_w, mod_i, wu_bf16, wd_bf16, is_ctx):
    tm, tf = MLP_ROW_TILE, 1024
    rows = r_seg.shape[0]
    if is_ctx:
        out_spec = pl.BlockSpec((tm, D_MODEL), lambda t, f: (t, 0))
        out_shape = jax.ShapeDtypeStruct((rows, D_MODEL), F32)
    else:
        assert SEQ == GRID_W * GRID_W and tm % GRID_W == 0
        groups = tm // GRID_W
        per_batch = GRID_W // groups
        out_spec = pl.BlockSpec((GRID_W, groups, D_MODEL), lambda t, f: (t // per_batch, t % per_batch, 0))
        out_shape = jax.ShapeDtypeStruct((rows // GRID_W, GRID_W, D_MODEL), F32)
    out = pl.pallas_call(
        functools.partial(_mlp_kernel, tm=tm, is_ctx=is_ctx),
        grid=(rows // tm, D_FF // tf),
        in_specs=[pl.BlockSpec((tm, D_MODEL), lambda t, f: (t, 0)),
                  pl.BlockSpec((1, D_MODEL), lambda t, f: (0, 0)),
                  pl.BlockSpec((MOD_ROWS, D_MODEL), lambda t, f: (0, 3)),
                  pl.BlockSpec((MOD_ROWS, D_MODEL), lambda t, f: (0, 4)),
                  pl.BlockSpec((MOD_ROWS, D_MODEL), lambda t, f: (0, 5)),
                  pl.BlockSpec((D_MODEL, tf), lambda t, f: (0, f)),
                  pl.BlockSpec((tf, D_MODEL), lambda t, f: (f, 0))],
        out_specs=out_spec,
        out_shape=out_shape,
        scratch_shapes=[pltpu.VMEM((tm, D_MODEL), BF16), pltpu.VMEM((tm, D_MODEL), F32)] + (
            [] if is_ctx else [pltpu.VMEM((D_MODEL // LANES, tm, LANES), F32)]),
        compiler_params=_params("arbitrary", "arbitrary"),
        name="mlp_ctx" if is_ctx else "mlp",
    )(r_seg, norm_w.reshape(1, D_MODEL), mod_i, mod_i, mod_i, wu_bf16, wd_bf16)
    return out.reshape(rows, D_MODEL)


def _final_norm_kernel(x_ref, w_ref, o_ref):
    x = x_ref[...]
    o_ref[...] = x * lax.rsqrt(jnp.mean(x * x, axis=-1, keepdims=True) + EPS) * w_ref[...]


def final_norm(r, w):
    tm = 512
    return pl.pallas_call(
        _final_norm_kernel,
        grid=(N_LAT // tm,),
        in_specs=[pl.BlockSpec((tm, D_MODEL), lambda t: (t, 0)),
                  pl.BlockSpec((1, D_MODEL), lambda t: (0, 0))],
        out_specs=pl.BlockSpec((tm, D_MODEL), lambda t: (t, 0)),
        out_shape=jax.ShapeDtypeStruct((N_LAT, D_MODEL), F32),
        compiler_params=_params("arbitrary"),
        name="final_norm",
    )(r, w.reshape(1, D_MODEL))


LAT_TILES = SEQ // SEQ_TILE
N_SEQ_TILES = T_ROWS // SEQ_TILE
HALO_BLOCKS = SEQ_TILE // SUBLANES


def _seg_first(t):
    return jnp.logical_or(t >= BATCH * LAT_TILES, t % LAT_TILES == 0)


def _seg_last(t):
    return jnp.logical_or(t >= BATCH * LAT_TILES, t % LAT_TILES == LAT_TILES - 1)


def _conv4(ext_ref, cw_ref, tm, cols):
    ext = ext_ref[:, cols]
    rows = ext.shape[0]
    body = slice(SUBLANES, SUBLANES + tm)
    tap = lambda shift: ext[body] if shift == 0 else pltpu.roll(ext, shift % rows, 0)[body]
    acc = cw_ref[0:1, cols] * tap(2)
    acc = acc + cw_ref[1:2, cols] * tap(1)
    acc = acc + cw_ref[2:3, cols] * tap(0)
    acc = acc + cw_ref[3:4, cols] * tap(-1)
    return acc


def _fill_halo(ext_ref, prev_ref, cur_ref, next_ref, t, tm):
    zero = jnp.zeros((SUBLANES, ext_ref.shape[1]), F32)
    ext_ref[pl.ds(0, SUBLANES), :] = jnp.where(_seg_first(t), zero, prev_ref[...])
    ext_ref[pl.ds(SUBLANES, tm), :] = cur_ref[...]
    ext_ref[pl.ds(SUBLANES + tm, SUBLANES), :] = jnp.where(_seg_last(t), zero, next_ref[...])


def _halo_specs(width, col_of):
    last_blk = T_ROWS // SUBLANES - 1
    return [pl.BlockSpec((SUBLANES, width), lambda t, *g: (jnp.maximum(t * HALO_BLOCKS - 1, 0), col_of(t, *g))),
            pl.BlockSpec((SEQ_TILE, width), lambda t, *g: (t, col_of(t, *g))),
            pl.BlockSpec((SUBLANES, width),
                         lambda t, *g: (jnp.minimum((t + 1) * HALO_BLOCKS, last_blk), col_of(t, *g)))]


def _lru_prep_kernel(py_ref, xp_ref, xc_ref, xn_ref, cw_ref, cb_ref, wg_ref, bg_ref, lam_ref,
                     y_ref, a_ref, u_ref, ext_ref):
    t = pl.program_id(0)
    tm = SEQ_TILE
    y_ref[...] = jax.nn.gelu(py_ref[...])
    _fill_halo(ext_ref, xp_ref, xc_ref, xn_ref, t, tm)
    for n in range(LRU_BLOCKS):
        cols = slice(n * LRU_BW, (n + 1) * LRU_BW)
        xr = _conv4(ext_ref, cw_ref, tm, cols) + cb_ref[:, cols]
        xb = xr.astype(BF16)
        for d in range(2):
            gt = [jnp.dot(xb, wg_ref[d, g, n], preferred_element_type=F32) + bg_ref[d, g, :, cols] for g in range(2)]
            log_a = -LRU_C * jax.nn.sigmoid(gt[0]) * jax.nn.softplus(-lam_ref[d, :, cols])
            a = jnp.exp(log_a)
            a_ref[d, :, cols] = a
            one_minus_a2 = -jnp.tanh(log_a) * (a * a + 1.0)
            u_ref[d, :, cols] = jnp.sqrt(one_minus_a2) * jax.nn.sigmoid(gt[1]) * xr


def lru_prep(p, conv_w, conv_b, w_gate_bf16, b_gate, lam):
    w = LRU_W
    full = lambda shape: pl.BlockSpec(shape, lambda t: (0,) * len(shape))
    return pl.pallas_call(
        _lru_prep_kernel,
        grid=(N_SEQ_TILES,),
        in_specs=[pl.BlockSpec((SEQ_TILE, w), lambda t: (t, 0))] + _halo_specs(w, lambda t: 1) + [
            full((CONV_W, w)), full((1, w)),
            full((2, 2, LRU_BLOCKS, LRU_BW, LRU_BW)), full((2, 2, 1, w)), full((2, 1, w))],
        out_specs=[pl.BlockSpec((SEQ_TILE, w), lambda t: (t, 0)),
                   pl.BlockSpec((2, SEQ_TILE, w), lambda t: (0, t, 0)),
                   pl.BlockSpec((2, SEQ_TILE, w), lambda t: (0, t, 0))],
        out_shape=[jax.ShapeDtypeStruct((T_ROWS, w), F32),
                   jax.ShapeDtypeStruct((2, T_ROWS, w), F32),
                   jax.ShapeDtypeStruct((2, T_ROWS, w), F32)],
        scratch_shapes=[pltpu.VMEM((SEQ_TILE + 2 * SUBLANES, w), F32)],
        compiler_params=_params("arbitrary"),
        name="lru_prep",
    )(p, p, p, p, conv_w, conv_b.reshape(1, w), w_gate_bf16, b_gate.reshape(2, 2, 1, w), lam.reshape(2, 1, w))


def _lru_scan_kernel(a_ref, u_ref, o_ref, carry_ref, *, reverse):
    @pl.when(pl.program_id(2) == 0)
    def _():
        carry_ref[...] = jnp.zeros_like(carry_ref)

    nblk = SEQ_TILE // SUBLANES
    width = a_ref.shape[-1]
    row = lax.broadcasted_iota(jnp.int32, (SUBLANES, width), 0)

    def body(i, carry):
        blk = (nblk - 1 - i) if reverse else i
        r0 = pl.multiple_of(blk * SUBLANES, SUBLANES)
        a = a_ref[0, pl.ds(r0, SUBLANES), :]
        u = u_ref[0, pl.ds(r0, SUBLANES), :]
        for s in (1, 2, 4):
            if reverse:
                keep = row < SUBLANES - s
                shift = SUBLANES - s
            else:
                keep = row >= s
                shift = s
            a_sh = jnp.where(keep, pltpu.roll(a, shift, 0), 1.0)
            u_sh = jnp.where(keep, pltpu.roll(u, shift, 0), 0.0)
            u = a * u_sh + u
            a = a * a_sh
        h = u + a * carry
        o_ref[pl.ds(r0, SUBLANES), :] = h
        return h[0:1, :] if reverse else h[SUBLANES - 1:SUBLANES, :]

    carry_ref[...] = lax.fori_loop(0, nblk, body, carry_ref[...])


def lru_scan(a, u, direction):
    reverse = direction == 1
    tw = LRU_W

    def row_tile(b, k):
        lat = b * LAT_TILES + ((LAT_TILES - k) if reverse else (k - 1))
        return jnp.where(k == 0, BATCH * LAT_TILES + b, lat)

    spec3 = pl.BlockSpec((1, SEQ_TILE, tw), lambda b, l, k: (direction, row_tile(b, k), l))
    return pl.pallas_call(
        functools.partial(_lru_scan_kernel, reverse=reverse),
        grid=(BATCH, LRU_W // tw, LAT_TILES + 1),
        in_specs=[spec3, spec3],
        out_specs=pl.BlockSpec((SEQ_TILE, tw), lambda b, l, k: (row_tile(b, k), l)),
        out_shape=jax.ShapeDtypeStruct((T_ROWS, LRU_W), F32),
        scratch_shapes=[pltpu.VMEM((1, tw), F32)],
        compiler_params=_params("arbitrary", "arbitrary", "arbitrary"),
        name="lru_scan_bwd" if reverse else "lru_scan_fwd",
    )(a, u)


def _lru_post_kernel(hf_ref, hb_ref, y_ref, o_ref):
    o_ref[...] = ((hf_ref[...] + hb_ref[...]) * y_ref[...]).astype(BF16)


def lru_post(hf, hb, y):
    tm = 512
    spec = pl.BlockSpec((tm, LRU_W), lambda t: (t, 0))
    return pl.pallas_call(
        _lru_post_kernel,
        grid=(T_ROWS // tm,),
        in_specs=[spec, spec, spec],
        out_specs=spec,
        out_shape=jax.ShapeDtypeStruct((T_ROWS, LRU_W), BF16),
        compiler_params=_params("arbitrary"),
        name="lru_post",
    )(hf, hb, y)


def lru_mixer(p, conv_w, conv_b, w_gate, b_gate, lam):
    y, a, u = lru_prep(p, conv_w, conv_b, w_gate.astype(BF16), b_gate, lam)
    return lru_post(lru_scan(a, u, 0), lru_scan(a, u, 1), y)


PAIR_SLOTS = 16
NEG_BIG = -1e30


def _split3(x):
    hi = x.astype(BF16)
    r1 = x - hi.astype(F32)
    mid = r1.astype(BF16)
    lo = (r1 - mid.astype(F32)).astype(BF16)
    return hi, mid, lo


def _gates_kernel(x_ref, pa_ref, pb_ref, o_ref, orow_ref, *, kind, npairs):
    x = x_ref[...]
    tm = x.shape[0]
    lane = lax.broadcasted_iota(jnp.int32, x.shape, 1)
    if kind == "gdn":
        first = jax.nn.sigmoid(x)
        second = -jnp.exp(pa_ref[...]) * jax.nn.softplus(x + pb_ref[...])
    else:
        gt = GATE_CAP * jnp.tanh((x + pb_ref[...]) / GATE_CAP)
        first = gt
        second = jax.nn.log_sigmoid(gt)
    r = lax.broadcasted_iota(jnp.int32, (tm, tm), 0)
    s = lax.broadcasted_iota(jnp.int32, (tm, tm), 1)
    same = (r // CHUNK) == (s // CHUNK)
    as_w = lambda m: jnp.where(m, 1.0, 0.0).astype(BF16)
    parts = _split3(second)

    def chunk_sum(w):
        acc = jnp.dot(w, parts[0], preferred_element_type=F32)
        acc = acc + jnp.dot(w, parts[1], preferred_element_type=F32)
        return acc + jnp.dot(w, parts[2], preferred_element_type=F32)

    prefix = chunk_sum(as_w(jnp.logical_and(same, s <= r)))
    suffix = chunk_sum(as_w(jnp.logical_and(same, s >= r)))
    total = chunk_sum(as_w(same))
    cs = jnp.where((lane & 3) >= 2, suffix, prefix)
    x1 = jnp.where((lane & 7) < 4, first, cs)
    row_q = cs if kind == "gdn" else pltpu.roll(first, 4, 1) - cs
    row_t = row_q.T
    for j in range(npairs):
        o_ref[j] = jnp.concatenate([x1[:, 8 * j:8 * j + 8], total[:, 8 * j:8 * j + 8]], axis=1)
        orow_ref[j] = row_t[8 * j + 4:8 * j + 8, :]


def gate_columns(gates, row_a, row_b, kind, npairs):
    p, col_block = gates, 0
    return pl.pallas_call(
        functools.partial(_gates_kernel, kind=kind, npairs=npairs),
        grid=(N_SEQ_TILES,),
        in_specs=[pl.BlockSpec((SEQ_TILE, LANES), lambda t: (t, col_block)),
                  pl.BlockSpec((1, LANES), lambda t: (0, 0)),
                  pl.BlockSpec((1, LANES), lambda t: (0, 0))],
        out_specs=[pl.BlockSpec((npairs, SEQ_TILE, PAIR_SLOTS), lambda t: (0, t, 0)),
                   pl.BlockSpec((npairs, 4, SEQ_TILE), lambda t: (0, 0, t))],
        out_shape=[jax.ShapeDtypeStruct((npairs, T_ROWS, PAIR_SLOTS), F32),
                   jax.ShapeDtypeStruct((npairs, 4, T_ROWS), F32)],
        compiler_params=_params("arbitrary"),
        name=kind + "_gates",
    )(p, row_a, row_b)


STACK = 4 * CHUNK
CHUNKS_PER_TILE = SEQ_TILE // CHUNK


def _mm(a, b):
    return jnp.dot(a.astype(BF16), b.astype(BF16), preferred_element_type=F32)


def _mm_nt(a, b):
    return lax.dot_general(a.astype(BF16), b.astype(BF16), (((1,), (1,)), ((), ())),
                           preferred_element_type=F32)


def _stack_masks():
    r = lax.broadcasted_iota(jnp.int32, (STACK, STACK), 0)
    s = lax.broadcasted_iota(jnp.int32, (STACK, STACK), 1)
    same = (r // CHUNK) == (s // CHUNK)
    fwd = r < 2 * CHUNK
    ordered = jnp.logical_or(jnp.logical_and(fwd, s <= r), jnp.logical_and(jnp.logical_not(fwd), s >= r))
    incl = jnp.logical_and(same, ordered)
    strict = jnp.logical_and(incl, r != s)
    return r, s, incl, strict


def _col4(gf, gb, slot):
    return jnp.concatenate([gf[:, slot:slot + 1], gf[:, slot + 1:slot + 2],
                            gb[:, slot + 2:slot + 3], gb[:, slot + 3:slot + 4]], axis=0)


def _diag_blocks(x, rows0, width):
    return jnp.concatenate([x[rows0 + c * CHUNK:rows0 + (c + 1) * CHUNK, c * width:(c + 1) * width]
                            for c in range(4)], axis=0)


def _block_diag_place(v, r_col):
    chain = r_col // CHUNK
    return jnp.concatenate([jnp.where(chain == c, v, 0.0) for c in range(4)], axis=1)


def _chain_scale(col, width):
    blocks = []
    for c in range(4):
        t = jnp.broadcast_to(col[c * CHUNK:(c + 1) * CHUNK], (CHUNK, width))
        blocks.append(jnp.concatenate([t, t], axis=0))
    return jnp.concatenate(blocks, axis=1)


def _state_products(x, y, state, width):
    xs, ys = [], []
    for p in range(2):
        rows = slice(2 * p * CHUNK, (2 * p + 2) * CHUNK)
        out = _mm(jnp.concatenate([x[rows], y[rows]], axis=0), state[:, 2 * p * width:(2 * p + 2) * width])
        for e in range(2):
            cols = slice(e * width, (e + 1) * width)
            xs.append(out[e * CHUNK:(e + 1) * CHUNK, cols])
            ys.append(out[(2 + e) * CHUNK:(3 + e) * CHUNK, cols])
    return jnp.concatenate(xs, axis=0), jnp.concatenate(ys, axis=0)


def _scan_tile(b, k, reverse):
    lat = b * LAT_TILES + ((LAT_TILES - k) if reverse else (k - 1))
    return jnp.where(k == 0, BATCH * LAT_TILES + b, lat)


def _gdn_conv_kernel(xp_ref, xc_ref, xn_ref, cw_ref, o_ref, ext_ref):
    t = pl.program_id(0)
    ct = pl.program_id(1)
    _fill_halo(ext_ref, xp_ref, xc_ref, xn_ref, t, SEQ_TILE)
    qk_tiles = GDN_QK // GDN_CONV_TILE

    def conv_silu(h):
        cols = slice(h * GDN_DK, (h + 1) * GDN_DK)
        y = _conv4(ext_ref, cw_ref, SEQ_TILE, cols)
        return cols, y * jax.nn.sigmoid(y)

    @pl.when(ct >= 2 * qk_tiles)
    def _():
        for h in range(GDN_CONV_TILE // GDN_DK):
            cols, y = conv_silu(h)
            o_ref[:, cols] = y

    @pl.when(ct < 2 * qk_tiles)
    def _():
        scale = jnp.where(ct < qk_tiles, GDN_DK ** -0.5, 1.0)
        for h in range(GDN_CONV_TILE // GDN_DK):
            cols, y = conv_silu(h)
            nrm = y * lax.rsqrt(jnp.sum(y * y, axis=-1, keepdims=True) + EPS)
            o_ref[:, cols] = nrm * scale


GDN_CONV_TILE = 1024


def gdn_conv(p, conv_w):
    n = 2 * GDN_QK + GDN_VW
    tn = GDN_CONV_TILE
    return pl.pallas_call(
        _gdn_conv_kernel,
        grid=(N_SEQ_TILES, n // tn),
        in_specs=_halo_specs(tn, lambda t, c: c) + [pl.BlockSpec((CONV_W, tn), lambda t, c: (0, c))],
        out_specs=pl.BlockSpec((SEQ_TILE, tn), lambda t, c: (t, c)),
        out_shape=jax.ShapeDtypeStruct((T_ROWS, n), F32),
        scratch_shapes=[pltpu.VMEM((SEQ_TILE + 2 * SUBLANES, tn), F32)],
        compiler_params=_params("arbitrary", "arbitrary"),
        name="gdn_conv",
    )(p, p, p, conv_w)


def _row4(rf_ref, rb_ref, hp, rf, rb):
    return jnp.concatenate([rf_ref[hp, 0:1, rf:rf + CHUNK], rf_ref[hp, 1:2, rf:rf + CHUNK],
                            rb_ref[hp, 2:3, rb:rb + CHUNK], rb_ref[hp, 3:4, rb:rb + CHUNK]], axis=1)


GDN_STEP_PAIRS = 4
GDN_GROUP_CHUNKS = 1
GDN_STAGE_LAG = 2


def _gdn_core_kernel(qf_ref, kf_ref, vf_ref, gf_ref, grf_ref, qb_ref, kb_ref, vb_ref, gb_ref, grb_ref,
                     of_ref, ob_ref, s_ref):
    @pl.when(pl.program_id(2) == 0)
    def _():
        s_ref[...] = jnp.zeros_like(s_ref)

    r, s, incl, strict = _stack_masks()
    eye = r == s
    blk8 = (r // 8) == (s // 8)
    blk16 = (r // 16) == (s // 16)
    blk32 = (r // 32) == (s // 32)
    r_col = lax.broadcasted_iota(jnp.int32, (STACK, 1), 0)

    pairs = range(GDN_STEP_PAIRS)
    steps = range(CHUNKS_PER_TILE)
    rows_f = [n * CHUNK for n in steps]
    rows_b = [(CHUNKS_PER_TILE - 1 - n) * CHUNK for n in steps]
    problems = [(hp, n) for n in steps for hp in pairs]
    each = lambda fn, *lists: [fn(*xs) for xs in zip(*lists)]

    def load(hp, n):
        rf, rb = rows_f[n], rows_b[n]
        dk = slice(hp * GDN_DK, (hp + 1) * GDN_DK)
        v_of = lambda e: slice((2 * hp + e) * GDN_DV, (2 * hp + e + 1) * GDN_DV)
        kf = kf_ref[pl.ds(rf, CHUNK), dk]
        kb = kb_ref[pl.ds(rb, CHUNK), dk]
        qf = qf_ref[pl.ds(rf, CHUNK), dk]
        qb = qb_ref[pl.ds(rb, CHUNK), dk]
        k4 = jnp.concatenate([kf, kf, kb, kb], axis=0)
        q4 = jnp.concatenate([qf, qf, qb, qb], axis=0)
        v4 = jnp.concatenate([vf_ref[pl.ds(rf, CHUNK), v_of(0)], vf_ref[pl.ds(rf, CHUNK), v_of(1)],
                              vb_ref[pl.ds(rb, CHUNK), v_of(0)], vb_ref[pl.ds(rb, CHUNK), v_of(1)]], axis=0)
        gf = gf_ref[hp, pl.ds(rf, CHUNK), :]
        gb = gb_ref[hp, pl.ds(rb, CHUNK), :]
        return (k4, q4, v4, _col4(gf, gb, 0), _col4(gf, gb, 4), _col4(gf, gb, 12),
                _row4(grf_ref, grb_ref, hp, rf, rb))

    def gram_of(k, q):
        f, b = slice(0, CHUNK), slice(2 * CHUNK, 3 * CHUNK)
        g = _mm_nt(jnp.concatenate([k[f], k[b], q[f], q[b]], axis=0), k)
        return jnp.concatenate([g[i * CHUNK:(i + 1) * CHUNK] for i in (0, 0, 1, 1, 2, 2, 3, 3)], axis=0)

    def merge_fn(blk, level_mask):
        half = blk // 2
        starts = range(0, STACK, blk)
        active = [s0 + (half if s0 < 2 * CHUNK else 0) for s0 in starts]

        def left(a, x):
            rows = jnp.concatenate([a[s0:s0 + half] for s0 in active], axis=0)
            return _mm(rows, jnp.where(level_mask, x, 0.0))

        def right(a, y):
            z = _mm(y, a)
            out = []
            for i, s0 in enumerate(starts):
                upd = a[active[i]:active[i] + half] - z[i * half:(i + 1) * half]
                keep = a[s0:s0 + half] if active[i] != s0 else a[s0 + half:s0 + blk]
                out += [keep, upd] if active[i] != s0 else [upd, keep]
            return jnp.concatenate(out, axis=0)

        return left, right

    merge16 = merge_fn(16, jnp.logical_and(blk16, jnp.logical_not(blk8)))
    merge32 = merge_fn(32, jnp.logical_and(blk32, jnp.logical_not(blk16)))
    merge64 = merge_fn(CHUNK, jnp.logical_not(blk32))

    def phase_a(chunks):
        idx = [(hp, n) for n in chunks for hp in pairs]
        v = {}

        def s_load():
            v["k4"], v["q4"], v["v4"], v["beta"], v["gc"], v["gl"], v["gc_row"] = zip(*[load(hp, n) for hp, n in idx])
            v["eg"] = each(jnp.exp, v["gc"])
            v["gram"] = each(gram_of, v["k4"], v["q4"])

        def s_nmat():
            decay = each(lambda g, g_row: jnp.exp(jnp.where(incl, g - g_row, NEG_BIG)), v["gc"], v["gc_row"])
            v["nmat"] = each(lambda gm, d, b: jnp.where(strict, gm[:STACK] * d * b, 0.0),
                             v["gram"], decay, v["beta"])
            v["a_qk"] = each(lambda gm, d: gm[STACK:] * d, v["gram"], decay)
            v["n0"] = each(lambda x: jnp.where(blk8, x, 0.0), v["nmat"])
            v["t"] = each(lambda x: jnp.where(eye, 1.0, 0.0) - x, v["n0"])

        def s_p2():
            v["p2"] = each(_mm, v["n0"], v["n0"])

        def s_p4():
            v["p4"] = each(_mm, v["p2"], v["p2"])
            v["t"] = each(lambda a, b: a + _mm(a, b), v["t"], v["p2"])

        def s_t8():
            v["t"] = each(lambda a, b: a + _mm(a, b), v["t"], v["p4"])

        def s_left(merge):
            return lambda: v.__setitem__("y", each(merge[0], v["t"], v["nmat"]))

        def s_right(merge):
            return lambda: v.__setitem__("t", each(merge[1], v["t"], v["y"]))

        def s_uw():
            v["uw"] = each(lambda a, x, k, b, e: _mm(a, jnp.concatenate([x * b, k * (b * e)], axis=1)),
                           v["t"], v["v4"], v["k4"], v["beta"], v["eg"])
            v["qe"] = each(lambda q, e: q * e, v["q4"], v["eg"])
            v["k_t"] = each(lambda k, g, gt: (k * jnp.exp(gt - g)).T, v["k4"], v["gc"], v["gl"])
            v["s_scale"] = each(lambda gt: _chain_scale(jnp.exp(gt), GDN_DV), v["gl"])

        stages = [s_load, s_nmat, s_p2, s_p4, s_t8, s_left(merge16), s_right(merge16), s_left(merge32),
                  s_right(merge32), s_left(merge64), s_right(merge64), s_uw]
        return v, stages, {n: i * GDN_STEP_PAIRS for i, n in enumerate(chunks)}

    s_all = [s_ref[hp] for hp in pairs]

    def phase_b(group, n):
        v, _, base = group
        tmp = {}

        def s_state():
            for hp in pairs:
                i = base[n] + hp
                ws, qs = _state_products(v["uw"][i][:, GDN_DV:], v["qe"][i], s_all[hp], GDN_DV)
                tmp[hp] = (v["uw"][i][:, :GDN_DV] - ws, qs)

        def s_update():
            for hp in pairs:
                i = base[n] + hp
                v_new, qs = tmp[hp]
                o4 = qs + _mm(v["a_qk"][i], v_new)
                s_all[hp] = s_all[hp] * v["s_scale"][i] + _mm(v["k_t"][i], _block_diag_place(v_new, r_col))
                rf, rb = rows_f[n], rows_b[n]
                for c, (o_ref, r0) in enumerate(((of_ref, rf), (of_ref, rf), (ob_ref, rb), (ob_ref, rb))):
                    col = (2 * hp + c % 2) * GDN_DV
                    o_ref[pl.ds(r0, CHUNK), col:col + GDN_DV] = o4[c * CHUNK:(c + 1) * CHUNK]

        return [s_state, s_update]

    slots = {}
    for g in range(0, CHUNKS_PER_TILE, GDN_GROUP_CHUNKS):
        chunks = list(steps)[g:g + GDN_GROUP_CHUNKS]
        group = phase_a(chunks)
        sequence = group[1] + [st for n in chunks for st in phase_b(group, n)]
        for k, st in enumerate(sequence):
            slots.setdefault((g // GDN_GROUP_CHUNKS) * GDN_STAGE_LAG + k, []).append(st)
    for slot in sorted(slots):
        for st in slots[slot]:
            st()
    for hp in pairs:
        s_ref[hp] = s_all[hp]


def gdn_core(qkv, gcols, grows):
    hp = GDN_STEP_PAIRS
    kcol = GDN_QK // (hp * GDN_DK)
    vcol = 2 * GDN_QK // (hp * 2 * GDN_DV)

    def specs(reverse):
        tile = lambda b, j, k: _scan_tile(b, k, reverse)
        return [pl.BlockSpec((SEQ_TILE, hp * GDN_DK), lambda b, j, k: (tile(b, j, k), j)),
                pl.BlockSpec((SEQ_TILE, hp * GDN_DK), lambda b, j, k: (tile(b, j, k), kcol + j)),
                pl.BlockSpec((SEQ_TILE, hp * 2 * GDN_DV), lambda b, j, k: (tile(b, j, k), vcol + j)),
                pl.BlockSpec((hp, SEQ_TILE, PAIR_SLOTS), lambda b, j, k: (j, tile(b, j, k), 0)),
                pl.BlockSpec((hp, 4, SEQ_TILE), lambda b, j, k: (j, 0, tile(b, j, k)))]

    out_spec = lambda reverse: pl.BlockSpec((SEQ_TILE, hp * 2 * GDN_DV),
                                            lambda b, j, k: (_scan_tile(b, k, reverse), j))
    return pl.pallas_call(
        _gdn_core_kernel,
        grid=(BATCH, GDN_QK_HEADS // hp, LAT_TILES + 1),
        in_specs=specs(False) + specs(True),
        out_specs=[out_spec(False), out_spec(True)],
        out_shape=[jax.ShapeDtypeStruct((T_ROWS, GDN_VW), F32)] * 2,
        scratch_shapes=[pltpu.VMEM((hp, GDN_DK, 4 * GDN_DV), F32)],
        compiler_params=_params("arbitrary", "arbitrary", "arbitrary"),
        name="gdn_core",
    )(qkv, qkv, qkv, gcols, grows, qkv, qkv, qkv, gcols, grows)


def _head_norm_gate_kernel(of_ref, ob_ref, z_ref, nw_ref, y_ref, *, head_dim, gate):
    for h in range(of_ref.shape[1] // head_dim):
        sl = slice(h * head_dim, (h + 1) * head_dim)
        o = of_ref[:, sl] + ob_ref[:, sl]
        y = o * lax.rsqrt(jnp.mean(o * o, axis=-1, keepdims=True) + EPS) * nw_ref[:, sl]
        z = z_ref[:, sl]
        g = z * jax.nn.sigmoid(z) if gate == "silu" else jax.nn.sigmoid(z)
        y_ref[:, sl] = (y * g).astype(BF16)


def head_norm_gate(o_f, o_b, p, z_col_block, norm_row, head_dim, gate):
    width = o_f.shape[1]
    tm, tn = 512, 512
    spec = pl.BlockSpec((tm, tn), lambda t, c: (t, c))
    return pl.pallas_call(
        functools.partial(_head_norm_gate_kernel, head_dim=head_dim, gate=gate),
        grid=(T_ROWS // tm, width // tn),
        in_specs=[spec, spec, pl.BlockSpec((tm, tn), lambda t, c: (t, z_col_block + c)),
                  pl.BlockSpec((1, tn), lambda t, c: (0, c))],
        out_specs=spec,
        out_shape=jax.ShapeDtypeStruct((T_ROWS, width), BF16),
        compiler_params=_params("arbitrary", "arbitrary"),
        name="head_norm_" + gate,
    )(o_f, o_b, p, norm_row)


def gdn_mixer(p, gates, conv_w, a_log, dt_bias, norm_w):
    zeros4 = jnp.zeros((GDN_QK_HEADS, 4), F32)
    per_pair = lambda t: t.reshape(2, GDN_QK_HEADS, 2).transpose(1, 0, 2).reshape(GDN_QK_HEADS, 4)
    row = lambda t: jnp.concatenate([zeros4, per_pair(t)], axis=1).reshape(1, LANES)
    gcols, grows = gate_columns(gates, row(a_log), row(dt_bias), "gdn", GDN_QK_HEADS)
    qkv = gdn_conv(p, conv_w)
    o_f, o_b = gdn_core(qkv, gcols, grows)
    norm_row = jnp.tile(norm_w, GDN_V_HEADS).reshape(1, GDN_VW)
    return head_norm_gate(o_f, o_b, p, (2 * GDN_QK + GDN_VW) // 512, norm_row, GDN_DV, "silu")


GDN_MAIN_COLS = 2 * GDN_QK + 2 * GDN_VW


def gdn_gate_weights(w_in):
    gates = w_in[:, GDN_MAIN_COLS:].reshape(D_MODEL, 2, 2, GDN_QK_HEADS, 2)
    return gates.transpose(0, 3, 1, 2, 4).reshape(D_MODEL, 4 * GDN_V_HEADS).astype(BF16)


ML_PAIRS = ML_HEADS // 2


def _mlstm_core_kernel(qf_ref, kf_ref, vf_ref, gf_ref, grf_ref, qb_ref, kb_ref, vb_ref, gb_ref, grb_ref,
                       of_ref, ob_ref, c_ref, n_ref, m_ref):
    @pl.when(pl.program_id(2) == 0)
    def _():
        c_ref[...] = jnp.zeros_like(c_ref)
        n_ref[...] = jnp.zeros_like(n_ref)
        m_ref[...] = jnp.zeros_like(m_ref)

    r, s, incl, _ = _stack_masks()
    r_col = lax.broadcasted_iota(jnp.int32, (STACK, 1), 0)
    last_rows = (CHUNK - 1, 2 * CHUNK - 1, 2 * CHUNK, 3 * CHUNK)

    steps = range(CHUNKS_PER_TILE)
    rows_f = [n * CHUNK for n in steps]
    rows_b = [(CHUNKS_PER_TILE - 1 - n) * CHUNK for n in steps]
    each = lambda fn, *lists: [fn(*xs) for xs in zip(*lists)]

    def load(rf, rb):
        def stack(f_ref, b_ref, w):
            return jnp.concatenate([f_ref[pl.ds(rf, CHUNK), 0:w], f_ref[pl.ds(rf, CHUNK), w:2 * w],
                                    b_ref[pl.ds(rb, CHUNK), 0:w], b_ref[pl.ds(rb, CHUNK), w:2 * w]], axis=0)

        gf = gf_ref[0, pl.ds(rf, CHUNK), :]
        gb = gb_ref[0, pl.ds(rb, CHUNK), :]
        return (stack(qf_ref, qb_ref, ML_DQK) * ML_DQK ** -0.5, stack(kf_ref, kb_ref, ML_DQK),
                stack(vf_ref, vb_ref, ML_DV), _col4(gf, gb, 0), _col4(gf, gb, 4), _col4(gf, gb, 12),
                _row4(grf_ref, grb_ref, 0, rf, rb))

    q4, k4, v4, ig, bc, bl, ib_row = zip(*each(load, rows_f, rows_b))
    d_log = each(lambda b, row: jnp.where(incl, b + row, NEG_BIG), bc, ib_row)
    row_max = each(lambda d: jnp.max(d, axis=1, keepdims=True), d_log)
    w_max = each(lambda rm: jnp.concatenate([jnp.broadcast_to(rm[i:i + 1], (CHUNK, 1)) for i in last_rows], axis=0),
                 row_max)
    qk = each(_mm_nt, q4, k4)

    m_in = [m_ref[...]]
    for n in steps:
        m_in.append(jnp.maximum(bl[n] + m_in[n], w_max[n]))
    m_out = m_in[1:]
    m_ref[...] = m_in[-1]

    inter = each(lambda b, m: b + m, bc, m_in)
    m_t = each(jnp.maximum, inter, row_max)
    smat = each(lambda x, d, mt: x * jnp.exp(d - mt), qk, d_log, m_t)
    w_inter = each(lambda i, mt: jnp.exp(i - mt), inter, m_t)
    intra = each(_mm, smat, v4)
    s_sum = each(lambda x: jnp.sum(x, axis=1, keepdims=True), smat)
    cd = each(lambda b, m0, m1: jnp.exp(b + m0 - m1), bl, m_in, m_out)
    kw = each(lambda k, b, c, i, m1: k * jnp.exp(b - c + i - m1), k4, bl, bc, ig, m_out)
    d_c = each(lambda x, v: _mm(x.T, _block_diag_place(v, r_col)), kw, v4)
    c_scale = each(lambda x: _chain_scale(x, ML_DV), cd)

    c_all = c_ref[...]
    n_all = [n_ref[c:c + 1, :] for c in range(4)]
    for n in steps:
        qc = []
        for p in range(2):
            out = _mm(q4[n][2 * p * CHUNK:(2 * p + 2) * CHUNK], c_all[:, 2 * p * ML_DV:(2 * p + 2) * ML_DV])
            qc += [out[e * CHUNK:(e + 1) * CHUNK, e * ML_DV:(e + 1) * ML_DV] for e in range(2)]
        qc = jnp.concatenate(qc, axis=0)
        num = w_inter[n] * qc + intra[n]
        n_rows = jnp.concatenate([jnp.broadcast_to(x, (CHUNK, ML_DQK)) for x in n_all], axis=0)
        den = w_inter[n] * jnp.sum(q4[n] * n_rows, axis=1, keepdims=True) + s_sum[n]
        h4 = num / jnp.maximum(jnp.abs(den), jnp.exp(-m_t[n]))
        c_all = c_all * c_scale[n] + d_c[n]
        n_all = [cd[n][c * CHUNK:c * CHUNK + 1] * n_all[c]
                 + jnp.sum(kw[n][c * CHUNK:(c + 1) * CHUNK], axis=0, keepdims=True) for c in range(4)]
        rf, rb = rows_f[n], rows_b[n]
        of_ref[pl.ds(rf, CHUNK), 0:ML_DV] = h4[0:CHUNK]
        of_ref[pl.ds(rf, CHUNK), ML_DV:2 * ML_DV] = h4[CHUNK:2 * CHUNK]
        ob_ref[pl.ds(rb, CHUNK), 0:ML_DV] = h4[2 * CHUNK:3 * CHUNK]
        ob_ref[pl.ds(rb, CHUNK), ML_DV:2 * ML_DV] = h4[3 * CHUNK:4 * CHUNK]
    c_ref[...] = c_all
    for c in range(4):
        n_ref[c:c + 1, :] = n_all[c]


def mlstm_core(p, gcols, grows):
    kcol = ML_QK // (2 * ML_DQK)
    vcol = 2 * ML_QK // (2 * ML_DV)

    def specs(reverse):
        tile = lambda b, j, k: _scan_tile(b, k, reverse)
        return [pl.BlockSpec((SEQ_TILE, 2 * ML_DQK), lambda b, j, k: (tile(b, j, k), j)),
                pl.BlockSpec((SEQ_TILE, 2 * ML_DQK), lambda b, j, k: (tile(b, j, k), kcol + j)),
                pl.BlockSpec((SEQ_TILE, 2 * ML_DV), lambda b, j, k: (tile(b, j, k), vcol + j)),
                pl.BlockSpec((1, SEQ_TILE, PAIR_SLOTS), lambda b, j, k: (j, tile(b, j, k), 0)),
                pl.BlockSpec((1, 4, SEQ_TILE), lambda b, j, k: (j, 0, tile(b, j, k)))]

    out_spec = lambda reverse: pl.BlockSpec((SEQ_TILE, 2 * ML_DV),
                                            lambda b, j, k: (_scan_tile(b, k, reverse), j))
    return pl.pallas_call(
        _mlstm_core_kernel,
        grid=(BATCH, ML_PAIRS, LAT_TILES + 1),
        in_specs=specs(False) + specs(True),
        out_specs=[out_spec(False), out_spec(True)],
        out_shape=[jax.ShapeDtypeStruct((T_ROWS, ML_V), F32)] * 2,
        scratch_shapes=[pltpu.VMEM((ML_DQK, 4 * ML_DV), F32), pltpu.VMEM((SUBLANES, ML_DQK), F32),
                        pltpu.VMEM((STACK, 1), F32)],
        compiler_params=_params("arbitrary", "arbitrary", "arbitrary"),
        name="mlstm_core",
    )(p, p, p, gcols, grows, p, p, p, gcols, grows)


def mlstm_mixer(p, gates, gate_b, norm_w):
    bias = gate_b.reshape(2, 2, ML_PAIRS, 2).transpose(2, 1, 0, 3).reshape(1, 4 * ML_HEADS)
    bias = jnp.pad(bias, ((0, 0), (0, LANES - 4 * ML_HEADS)))
    gcols, grows = gate_columns(gates, bias, bias, "mlstm", ML_PAIRS)
    h_f, h_b = mlstm_core(p, gcols, grows)
    return head_norm_gate(h_f, h_b, p, (2 * ML_QK + ML_V) // 512, norm_w.reshape(1, ML_V), ML_DV, "sigmoid")


def mlstm_in_weights(w_in):
    base = 2 * ML_QK + 2 * ML_V
    gates = w_in[:, base:].reshape(D_MODEL, 2, 2, ML_PAIRS, 2)
    gates = gates.transpose(0, 3, 2, 1, 4).reshape(D_MODEL, 4 * ML_HEADS)
    return w_in[:, :base].astype(BF16), _pad_cols(gates, LANES).astype(BF16)


def _pad_cols(w, mult):
    n = w.shape[1]
    return jnp.pad(w, ((0, 0), (0, (-n) % mult)))


def kernel(x, c, ctx, c_ctx, w_mod, b_mod, norm_mix, norm_ff, norm_out, gdn_w_in, gdn_conv, gdn_a_log, gdn_dt_bias, gdn_norm, gdn_w_out, ml_w_in, ml_gate_b, ml_norm, ml_w_out, lru_w_in, lru_conv, lru_conv_b, lru_w_gate, lru_b_gate, lru_lambda, lru_w_out, ff_w_up, ff_w_down):
    assert DEPTH % 2 == 0
    r_lat = x.reshape(N_LAT, D_MODEL)
    r_ctx = ctx.reshape(N_CTX, D_MODEL)
    c_all = jnp.concatenate([c, c_ctx[None], jnp.zeros((MOD_ROWS - BATCH - 1, D_MODEL), F32)], axis=0)
    mod = modulation_table(c_all, w_mod, b_mod)
    gdn_w_in_bf16 = gdn_w_in.astype(BF16)
    for i in range(DEPTH):
        kind, j = i % N_MIXERS, i // N_MIXERS
        if kind == 0:
            p, gates = in_proj(r_lat, r_ctx, norm_mix[i], mod[i], gdn_w_in_bf16, gdn_gate_weights(gdn_w_in[j]),
                               layer=j, n=GDN_MAIN_COLS)
            y = gdn_mixer(p, gates, gdn_conv[j], gdn_a_log[j], gdn_dt_bias[j], gdn_norm[j])
            w_out = gdn_w_out[j]
        elif kind == 1:
            p, gates = in_proj(r_lat, r_ctx, norm_mix[i], mod[i], *mlstm_in_weights(ml_w_in[j]))
            y = mlstm_mixer(p, gates, ml_gate_b[j], ml_norm[j])
            w_out = ml_w_out[j]
        else:
            no_gates = jnp.zeros((D_MODEL, LANES), BF16)
            p, _ = in_proj(r_lat, r_ctx, norm_mix[i], mod[i], lru_w_in[j].astype(BF16), no_gates)
            y = lru_mixer(p, lru_conv[j], lru_conv_b[j], lru_w_gate[j], lru_b_gate[j], lru_lambda[j])
            w_out = lru_w_out[j]
        w_out, w_up, w_down = w_out.astype(BF16), ff_w_up[i].astype(BF16), ff_w_down[i].astype(BF16)
        r_lat = out_proj(y, w_out, r_lat, mod[i], is_ctx=False)
        r_lat = mlp(r_lat, norm_ff[i], mod[i], w_up, w_down, is_ctx=False)
        if i < DEPTH - 1:
            r_ctx = out_proj(y, w_out, r_ctx, mod[i], is_ctx=True)
            r_ctx = mlp(r_ctx, norm_ff[i], mod[i], w_up, w_down, is_ctx=True)
    return final_norm(r_lat, norm_out).reshape(BATCH, SEQ, D_MODEL)
```

```python
import functools
import math

import jax
import jax.numpy as jnp
from jax import lax
from jax.experimental import pallas as pl
from jax.experimental.pallas import tpu as pltpu

F32 = jnp.float32
BF16 = jnp.bfloat16

D_MODEL = 2048
BATCH = 4
SEQ = 4096
DEPTH = 4
CTX_LEN = 256
GRID_W = 64
N_MIXERS = 3
CHUNK = 64
CONV_W = 4
EPS = 1e-6
D_FF = 4 * D_MODEL

GDN_QK_HEADS = D_MODEL // 128
GDN_V_HEADS = 2 * GDN_QK_HEADS
GDN_DK = 128
GDN_DV = 128
GDN_QK = GDN_QK_HEADS * GDN_DK
GDN_VW = GDN_V_HEADS * GDN_DV
GDN_REP = GDN_V_HEADS // GDN_QK_HEADS
GDN_IN = 2 * GDN_QK + 2 * GDN_VW + 4 * GDN_V_HEADS

ML_HEADS = 8
ML_DQK = D_MODEL // (2 * ML_HEADS)
ML_DV = D_MODEL // ML_HEADS
ML_QK = ML_HEADS * ML_DQK
ML_V = ML_HEADS * ML_DV
ML_IN = 2 * ML_QK + 2 * ML_V + 4 * ML_HEADS
GATE_CAP = 15.0

LRU_W = D_MODEL
LRU_BLOCKS = 8
LRU_BW = LRU_W // LRU_BLOCKS
LRU_C = 8.0

N_LAT = BATCH * SEQ
N_CTX = BATCH * CTX_LEN
T_ROWS = N_LAT + N_CTX
MOD_ROWS = 8
CTX_MOD_ROW = BATCH

V7X_VMEM_BYTES = 64 * 1024 * 1024
VMEM_LIMIT = 56 * 1024 * 1024
SUBLANES = 8
LANES = 128

ROW_TILE = 1024
MLP_ROW_TILE = 512
COL_TILE = 512
IN_PROJ_MAX_COL_TILE = 1280
SEQ_TILE = 256


def _params(*sem):
    return pltpu.CompilerParams(dimension_semantics=sem, vmem_limit_bytes=VMEM_LIMIT)


def _mod_row(tile, tile_rows):
    lat_tiles = N_LAT // tile_rows
    return jnp.where(tile < lat_tiles, tile // (SEQ // tile_rows), CTX_MOD_ROW)


def _mod_kernel(c_ref, w_ref, b_ref, o_ref):
    c = c_ref[...]
    s = c * jax.nn.sigmoid(c)
    o_ref[0] = jnp.dot(s.astype(BF16), w_ref[0].astype(BF16), preferred_element_type=F32) + b_ref[0]


def modulation_table(c_all, w_mod, b_mod):
    tn = 1024
    n6 = 6 * D_MODEL
    return pl.pallas_call(
        _mod_kernel,
        grid=(DEPTH, n6 // tn),
        in_specs=[pl.BlockSpec((MOD_ROWS, D_MODEL), lambda i, n: (0, 0)),
                  pl.BlockSpec((1, D_MODEL, tn), lambda i, n: (i, 0, n)),
                  pl.BlockSpec((1, 1, tn), lambda i, n: (i, 0, n))],
        out_specs=pl.BlockSpec((1, MOD_ROWS, tn), lambda i, n: (i, 0, n)),
        out_shape=jax.ShapeDtypeStruct((DEPTH, MOD_ROWS, n6), F32),
        compiler_params=_params("arbitrary", "arbitrary"),
        name="modulation_table",
    )(c_all, w_mod, b_mod.reshape(DEPTH, 1, n6))


def _norm_modulate(x, nw, shift, scale):
    y = x * lax.rsqrt(jnp.mean(x * x, axis=-1, keepdims=True) + EPS)
    return (y * nw) * (1.0 + scale) + shift


def _in_proj_kernel(xl_ref, xc_ref, nw_ref, sh_ref, sc_ref, w_ref, wg_ref, o_ref, og_ref, h_ref, *, tm):
    t = pl.program_id(0)
    row = _mod_row(t, tm)
    lat_tiles = N_LAT // tm

    def prologue(x_ref):
        h = _norm_modulate(x_ref[...], nw_ref[...], sh_ref[pl.ds(row, 1), :], sc_ref[pl.ds(row, 1), :])
        h_ref[...] = h.astype(BF16)
        og_ref[...] = jnp.dot(h_ref[...], wg_ref[...], preferred_element_type=F32)

    @pl.when(jnp.logical_and(pl.program_id(1) == 0, t < lat_tiles))
    def _():
        prologue(xl_ref)

    @pl.when(jnp.logical_and(pl.program_id(1) == 0, t >= lat_tiles))
    def _():
        prologue(xc_ref)

    o_ref[...] = jnp.dot(h_ref[...], w_ref[...], preferred_element_type=F32)


def in_proj(r_lat, r_ctx, norm_w, mod_i, w_bf16, w_gates_bf16, layer=None, n=None):
    n = w_bf16.shape[-1] if n is None else n
    w_spec = (lambda tn: pl.BlockSpec((D_MODEL, tn), lambda t, j: (0, j))) if layer is None else (
        lambda tn: pl.BlockSpec((None, D_MODEL, tn), lambda t, j: (layer, 0, j)))
    tm = ROW_TILE
    tn = max(d * LANES for d in range(1, IN_PROJ_MAX_COL_TILE // LANES + 1) if (n // LANES) % d == 0)
    lat_tiles = N_LAT // tm
    assert N_CTX == tm and n % LANES == 0
    once = pl.Buffered(1)
    return pl.pallas_call(
        functools.partial(_in_proj_kernel, tm=tm),
        grid=(T_ROWS // tm, n // tn),
        in_specs=[pl.BlockSpec((tm, D_MODEL), lambda t, j: (jnp.minimum(t, lat_tiles - 1), 0), pipeline_mode=once),
                  pl.BlockSpec((tm, D_MODEL), lambda t, j: (0, 0), pipeline_mode=once),
                  pl.BlockSpec((1, D_MODEL), lambda t, j: (0, 0)),
                  pl.BlockSpec((MOD_ROWS, D_MODEL), lambda t, j: (0, 0)),
                  pl.BlockSpec((MOD_ROWS, D_MODEL), lambda t, j: (0, 1)),
                  w_spec(tn),
                  pl.BlockSpec((D_MODEL, LANES), lambda t, j: (0, 0))],
        out_specs=[pl.BlockSpec((tm, tn), lambda t, j: (t, j)),
                   pl.BlockSpec((tm, LANES), lambda t, j: (t, 0))],
        out_shape=[jax.ShapeDtypeStruct((T_ROWS, n), F32), jax.ShapeDtypeStruct((T_ROWS, LANES), F32)],
        scratch_shapes=[pltpu.VMEM((tm, D_MODEL), BF16)],
        compiler_params=_params("arbitrary", "arbitrary"),
        name="in_proj",
    )(r_lat, r_ctx, norm_w.reshape(1, D_MODEL), mod_i, mod_i, w_bf16, w_gates_bf16)


def _segment_mod_row(tile, tile_rows, is_ctx):
    return CTX_MOD_ROW if is_ctx else tile // (SEQ // tile_rows)


def _out_proj_kernel(y_ref, w_ref, r_ref, g_ref, o_ref, *, tm, is_ctx):
    row = _segment_mod_row(pl.program_id(0), tm, is_ctx)
    acc = jnp.dot(y_ref[...], w_ref[...], preferred_element_type=F32)
    o_ref[...] = r_ref[...] + g_ref[pl.ds(row, 1), :] * acc


def out_proj(y_bf16, w_bf16, r_seg, mod_i, is_ctx):
    k = y_bf16.shape[1]
    tm, tn = ROW_TILE, COL_TILE
    nj = D_MODEL // tn
    rows = r_seg.shape[0]
    tile0 = N_LAT // tm if is_ctx else 0
    return pl.pallas_call(
        functools.partial(_out_proj_kernel, tm=tm, is_ctx=is_ctx),
        grid=(rows // tm, nj),
        in_specs=[pl.BlockSpec((tm, k), lambda t, j: (tile0 + t, 0)),
                  pl.BlockSpec((k, tn), lambda t, j: (0, j)),
                  pl.BlockSpec((tm, tn), lambda t, j: (t, j)),
                  pl.BlockSpec((MOD_ROWS, tn), lambda t, j: (0, 2 * nj + j))],
        out_specs=pl.BlockSpec((tm, tn), lambda t, j: (t, j)),
        out_shape=jax.ShapeDtypeStruct((rows, D_MODEL), F32),
        compiler_params=_params("arbitrary", "arbitrary"),
        name="out_proj_ctx" if is_ctx else "out_proj",
    )(y_bf16, w_bf16, r_seg, mod_i)


def _mlp_kernel(x_ref, nw_ref, sh_ref, sc_ref, g_ref, wu_ref, wd_ref, o_ref, h_ref, acc_ref, *maybe_tr_ref,
                tm, is_ctx):
    tr_ref = None if is_ctx else maybe_tr_ref[0]
    f = pl.program_id(1)
    row = _segment_mod_row(pl.program_id(0), tm, is_ctx)

    @pl.when(f == 0)
    def _():
        h = _norm_modulate(x_ref[...], nw_ref[...], sh_ref[pl.ds(row, 1), :], sc_ref[pl.ds(row, 1), :])
        h_ref[...] = h.astype(BF16)
        acc_ref[...] = jnp.zeros_like(acc_ref)

    u = jnp.dot(h_ref[...], wu_ref[...], preferred_element_type=F32)
    a = jnp.square(jnp.maximum(u, 0.0))
    acc_ref[...] += jnp.dot(a.astype(BF16), wd_ref[...], preferred_element_type=F32)

    @pl.when(f == pl.num_programs(1) - 1)
    def _():
        res = x_ref[...] + g_ref[pl.ds(row, 1), :] * acc_ref[...]
        if is_ctx:
            o_ref[...] = res
        else:
            lane_tiles = D_MODEL // LANES
            for c in range(lane_tiles):
                tr_ref[c] = res[:, c * LANES:(c + 1) * LANES]
            for b in range(GRID_W):
                for c in range(lane_tiles):
                    o_ref[b, :, c * LANES:(c + 1) * LANES] = tr_ref.at[c][pl.ds(b, tm // GRID_W, stride=GRID_W), :]


def mlp(r_seg, norm_w, mod_i, wu_bf16, wd_bf16, is_ctx):
    tm, tf = MLP_ROW_TILE, 1024
    rows = r_seg.shape[0]
    if is_ctx:
        out_spec = pl.BlockSpec((tm, D_MODEL), lambda t, f: (t, 0))
        out_shape = jax.ShapeDtypeStruct((rows, D_MODEL), F32)
    else:
        assert SEQ == GRID_W * GRID_W and tm % GRID_W == 0
        groups = tm // GRID_W
        per_batch = GRID_W // groups
        out_spec = pl.BlockSpec((GRID_W, groups, D_MODEL), lambda t, f: (t // per_batch, t % per_batch, 0))
        out_shape = jax.ShapeDtypeStruct((rows // GRID_W, GRID_W, D_MODEL), F32)
    out = pl.pallas_call(
        functools.partial(_mlp_kernel, tm=tm, is_ctx=is_ctx),
        grid=(rows // tm, D_FF // tf),
        in_specs=[pl.BlockSpec((tm, D_MODEL), lambda t, f: (t, 0)),
                  pl.BlockSpec((1, D_MODEL), lambda t, f: (0, 0)),
                  pl.BlockSpec((MOD_ROWS, D_MODEL), lambda t, f: (0, 3)),
                  pl.BlockSpec((MOD_ROWS, D_MODEL), lambda t, f: (0, 4)),
                  pl.BlockSpec((MOD_ROWS, D_MODEL), lambda t, f: (0, 5)),
                  pl.BlockSpec((D_MODEL, tf), lambda t, f: (0, f)),
                  pl.BlockSpec((tf, D_MODEL), lambda t, f: (f, 0))],
        out_specs=out_spec,
        out_shape=out_shape,
        scratch_shapes=[pltpu.VMEM((tm, D_MODEL), BF16), pltpu.VMEM((tm, D_MODEL), F32)] + (
            [] if is_ctx else [pltpu.VMEM((D_MODEL // LANES, tm, LANES), F32)]),
        compiler_params=_params("arbitrary", "arbitrary"),
        name="mlp_ctx" if is_ctx else "mlp",
    )(r_seg, norm_w.reshape(1, D_MODEL), mod_i, mod_i, mod_i, wu_bf16, wd_bf16)
    return out.reshape(rows, D_MODEL)


def _final_norm_kernel(x_ref, w_ref, o_ref):
    x = x_ref[...]
    o_ref[...] = x * lax.rsqrt(jnp.mean(x * x, axis=-1, keepdims=True) + EPS) * w_ref[...]


def final_norm(r, w):
    tm = 512
    return pl.pallas_call(
        _final_norm_kernel,
        grid=(N_LAT // tm,),
        in_specs=[pl.BlockSpec((tm, D_MODEL), lambda t: (t, 0)),
                  pl.BlockSpec((1, D_MODEL), lambda t: (0, 0))],
        out_specs=pl.BlockSpec((tm, D_MODEL), lambda t: (t, 0)),
        out_shape=jax.ShapeDtypeStruct((N_LAT, D_MODEL), F32),
        compiler_params=_params("arbitrary"),
        name="final_norm",
    )(r, w.reshape(1, D_MODEL))


LAT_TILES = SEQ // SEQ_TILE
N_SEQ_TILES = T_ROWS // SEQ_TILE
HALO_BLOCKS = SEQ_TILE // SUBLANES


def _seg_first(t):
    return jnp.logical_or(t >= BATCH * LAT_TILES, t % LAT_TILES == 0)


def _seg_last(t):
    return jnp.logical_or(t >= BATCH * LAT_TILES, t % LAT_TILES == LAT_TILES - 1)


def _conv4(ext_ref, cw_ref, tm, cols):
    ext = ext_ref[:, cols]
    rows = ext.shape[0]
    body = slice(SUBLANES, SUBLANES + tm)
    tap = lambda shift: ext[body] if shift == 0 else pltpu.roll(ext, shift % rows, 0)[body]
    acc = cw_ref[0:1, cols] * tap(2)
    acc = acc + cw_ref[1:2, cols] * tap(1)
    acc = acc + cw_ref[2:3, cols] * tap(0)
    acc = acc + cw_ref[3:4, cols] * tap(-1)
    return acc


def _fill_halo(ext_ref, prev_ref, cur_ref, next_ref, t, tm):
    zero = jnp.zeros((SUBLANES, ext_ref.shape[1]), F32)
    ext_ref[pl.ds(0, SUBLANES), :] = jnp.where(_seg_first(t), zero, prev_ref[...])
    ext_ref[pl.ds(SUBLANES, tm), :] = cur_ref[...]
    ext_ref[pl.ds(SUBLANES + tm, SUBLANES), :] = jnp.where(_seg_last(t), zero, next_ref[...])


def _halo_specs(width, col_of):
    last_blk = T_ROWS // SUBLANES - 1
    return [pl.BlockSpec((SUBLANES, width), lambda t, *g: (jnp.maximum(t * HALO_BLOCKS - 1, 0), col_of(t, *g))),
            pl.BlockSpec((SEQ_TILE, width), lambda t, *g: (t, col_of(t, *g))),
            pl.BlockSpec((SUBLANES, width),
                         lambda t, *g: (jnp.minimum((t + 1) * HALO_BLOCKS, last_blk), col_of(t, *g)))]


def _lru_prep_kernel(py_ref, xp_ref, xc_ref, xn_ref, cw_ref, cb_ref, wg_ref, bg_ref, lam_ref,
                     y_ref, a_ref, u_ref, ext_ref):
    t = pl.program_id(0)
    tm = SEQ_TILE
    y_ref[...] = jax.nn.gelu(py_ref[...])
    _fill_halo(ext_ref, xp_ref, xc_ref, xn_ref, t, tm)
    for n in range(LRU_BLOCKS):
        cols = slice(n * LRU_BW, (n + 1) * LRU_BW)
        xr = _conv4(ext_ref, cw_ref, tm, cols) + cb_ref[:, cols]
        xb = xr.astype(BF16)
        for d in range(2):
            gt = [jnp.dot(xb, wg_ref[d, g, n], preferred_element_type=F32) + bg_ref[d, g, :, cols] for g in range(2)]
            log_a = -LRU_C * jax.nn.sigmoid(gt[0]) * jax.nn.softplus(-lam_ref[d, :, cols])
            a = jnp.exp(log_a)
            a_ref[d, :, cols] = a
            one_minus_a2 = -jnp.tanh(log_a) * (a * a + 1.0)
            u_ref[d, :, cols] = jnp.sqrt(one_minus_a2) * jax.nn.sigmoid(gt[1]) * xr


def lru_prep(p, conv_w, conv_b, w_gate_bf16, b_gate, lam):
    w = LRU_W
    full = lambda shape: pl.BlockSpec(shape, lambda t: (0,) * len(shape))
    return pl.pallas_call(
        _lru_prep_kernel,
        grid=(N_SEQ_TILES,),
        in_specs=[pl.BlockSpec((SEQ_TILE, w), lambda t: (t, 0))] + _halo_specs(w, lambda t: 1) + [
            full((CONV_W, w)), full((1, w)),
            full((2, 2, LRU_BLOCKS, LRU_BW, LRU_BW)), full((2, 2, 1, w)), full((2, 1, w))],
        out_specs=[pl.BlockSpec((SEQ_TILE, w), lambda t: (t, 0)),
                   pl.BlockSpec((2, SEQ_TILE, w), lambda t: (0, t, 0)),
                   pl.BlockSpec((2, SEQ_TILE, w), lambda t: (0, t, 0))],
        out_shape=[jax.ShapeDtypeStruct((T_ROWS, w), F32),
                   jax.ShapeDtypeStruct((2, T_ROWS, w), F32),
                   jax.ShapeDtypeStruct((2, T_ROWS, w), F32)],
        scratch_shapes=[pltpu.VMEM((SEQ_TILE + 2 * SUBLANES, w), F32)],
        compiler_params=_params("arbitrary"),
        name="lru_prep",
    )(p, p, p, p, conv_w, conv_b.reshape(1, w), w_gate_bf16, b_gate.reshape(2, 2, 1, w), lam.reshape(2, 1, w))


def _lru_scan_kernel(a_ref, u_ref, o_ref, carry_ref, *, reverse):
    @pl.when(pl.program_id(2) == 0)
    def _():
        carry_ref[...] = jnp.zeros_like(carry_ref)

    nblk = SEQ_TILE // SUBLANES
    width = a_ref.shape[-1]
    row = lax.broadcasted_iota(jnp.int32, (SUBLANES, width), 0)

    def body(i, carry):
        blk = (nblk - 1 - i) if reverse else i
        r0 = pl.multiple_of(blk * SUBLANES, SUBLANES)
        a = a_ref[0, pl.ds(r0, SUBLANES), :]
        u = u_ref[0, pl.ds(r0, SUBLANES), :]
        for s in (1, 2, 4):
            if reverse:
                keep = row < SUBLANES - s
                shift = SUBLANES - s
            else:
                keep = row >= s
                shift = s
            a_sh = jnp.where(keep, pltpu.roll(a, shift, 0), 1.0)
            u_sh = jnp.where(keep, pltpu.roll(u, shift, 0), 0.0)
            u = a * u_sh + u
            a = a * a_sh
        h = u + a * carry
        o_ref[pl.ds(r0, SUBLANES), :] = h
        return h[0:1, :] if reverse else h[SUBLANES - 1:SUBLANES, :]

    carry_ref[...] = lax.fori_loop(0, nblk, body, carry_ref[...])


def lru_scan(a, u, direction):
    reverse = direction == 1
    tw = LRU_W

    def row_tile(b, k):
        lat = b * LAT_TILES + ((LAT_TILES - k) if reverse else (k - 1))
        return jnp.where(k == 0, BATCH * LAT_TILES + b, lat)

    spec3 = pl.BlockSpec((1, SEQ_TILE, tw), lambda b, l, k: (direction, row_tile(b, k), l))
    return pl.pallas_call(
        functools.partial(_lru_scan_kernel, reverse=reverse),
        grid=(BATCH, LRU_W // tw, LAT_TILES + 1),
        in_specs=[spec3, spec3],
        out_specs=pl.BlockSpec((SEQ_TILE, tw), lambda b, l, k: (row_tile(b, k), l)),
        out_shape=jax.ShapeDtypeStruct((T_ROWS, LRU_W), F32),
        scratch_shapes=[pltpu.VMEM((1, tw), F32)],
        compiler_params=_params("arbitrary", "arbitrary", "arbitrary"),
        name="lru_scan_bwd" if reverse else "lru_scan_fwd",
    )(a, u)


def _lru_post_kernel(hf_ref, hb_ref, y_ref, o_ref):
    o_ref[...] = ((hf_ref[...] + hb_ref[...]) * y_ref[...]).astype(BF16)


def lru_post(hf, hb, y):
    tm = 512
    spec = pl.BlockSpec((tm, LRU_W), lambda t: (t, 0))
    return pl.pallas_call(
        _lru_post_kernel,
        grid=(T_ROWS // tm,),
        in_specs=[spec, spec, spec],
        out_specs=spec,
        out_shape=jax.ShapeDtypeStruct((T_ROWS, LRU_W), BF16),
        compiler_params=_params("arbitrary"),
        name="lru_post",
    )(hf, hb, y)


def lru_mixer(p, conv_w, conv_b, w_gate, b_gate, lam):
    y, a, u = lru_prep(p, conv_w, conv_b, w_gate.astype(BF16), b_gate, lam)
    return lru_post(lru_scan(a, u, 0), lru_scan(a, u, 1), y)


PAIR_SLOTS = 16
NEG_BIG = -1e30


def _split3(x):
    hi = x.astype(BF16)
    r1 = x - hi.astype(F32)
    mid = r1.astype(BF16)
    lo = (r1 - mid.astype(F32)).astype(BF16)
    return hi, mid, lo


def _gates_kernel(x_ref, pa_ref, pb_ref, o_ref, orow_ref, *, kind, npairs):
    x = x_ref[...]
    tm = x.shape[0]
    lane = lax.broadcasted_iota(jnp.int32, x.shape, 1)
    if kind == "gdn":
        first = jax.nn.sigmoid(x)
        second = -jnp.exp(pa_ref[...]) * jax.nn.softplus(x + pb_ref[...])
    else:
        gt = GATE_CAP * jnp.tanh((x + pb_ref[...]) / GATE_CAP)
        first = gt
        second = jax.nn.log_sigmoid(gt)
    r = lax.broadcasted_iota(jnp.int32, (tm, tm), 0)
    s = lax.broadcasted_iota(jnp.int32, (tm, tm), 1)
    same = (r // CHUNK) == (s // CHUNK)
    as_w = lambda m: jnp.where(m, 1.0, 0.0).astype(BF16)
    parts = _split3(second)

    def chunk_sum(w):
        acc = jnp.dot(w, parts[0], preferred_element_type=F32)
        acc = acc + jnp.dot(w, parts[1], preferred_element_type=F32)
        return acc + jnp.dot(w, parts[2], preferred_element_type=F32)

    prefix = chunk_sum(as_w(jnp.logical_and(same, s <= r)))
    suffix = chunk_sum(as_w(jnp.logical_and(same, s >= r)))
    total = chunk_sum(as_w(same))
    cs = jnp.where((lane & 3) >= 2, suffix, prefix)
    x1 = jnp.where((lane & 7) < 4, first, cs)
    row_q = cs if kind == "gdn" else pltpu.roll(first, 4, 1) - cs
    row_t = row_q.T
    for j in range(npairs):
        o_ref[j] = jnp.concatenate([x1[:, 8 * j:8 * j + 8], total[:, 8 * j:8 * j + 8]], axis=1)
        orow_ref[j] = row_t[8 * j + 4:8 * j + 8, :]


def gate_columns(gates, row_a, row_b, kind, npairs):
    p, col_block = gates, 0
    return pl.pallas_call(
        functools.partial(_gates_kernel, kind=kind, npairs=npairs),
        grid=(N_SEQ_TILES,),
        in_specs=[pl.BlockSpec((SEQ_TILE, LANES), lambda t: (t, col_block)),
                  pl.BlockSpec((1, LANES), lambda t: (0, 0)),
                  pl.BlockSpec((1, LANES), lambda t: (0, 0))],
        out_specs=[pl.BlockSpec((npairs, SEQ_TILE, PAIR_SLOTS), lambda t: (0, t, 0)),
                   pl.BlockSpec((npairs, 4, SEQ_TILE), lambda t: (0, 0, t))],
        out_shape=[jax.ShapeDtypeStruct((npairs, T_ROWS, PAIR_SLOTS), F32),
                   jax.ShapeDtypeStruct((npairs, 4, T_ROWS), F32)],
        compiler_params=_params("arbitrary"),
        name=kind + "_gates",
    )(p, row_a, row_b)


STACK = 4 * CHUNK
CHUNKS_PER_TILE = SEQ_TILE // CHUNK


def _mm(a, b):
    return jnp.dot(a.astype(BF16), b.astype(BF16), preferred_element_type=F32)


def _mm_nt(a, b):
    return lax.dot_general(a.astype(BF16), b.astype(BF16), (((1,), (1,)), ((), ())),
                           preferred_element_type=F32)


def _stack_masks():
    r = lax.broadcasted_iota(jnp.int32, (STACK, STACK), 0)
    s = lax.broadcasted_iota(jnp.int32, (STACK, STACK), 1)
    same = (r // CHUNK) == (s // CHUNK)
    fwd = r < 2 * CHUNK
    ordered = jnp.logical_or(jnp.logical_and(fwd, s <= r), jnp.logical_and(jnp.logical_not(fwd), s >= r))
    incl = jnp.logical_and(same, ordered)
    strict = jnp.logical_and(incl, r != s)
    return r, s, incl, strict


def _col4(gf, gb, slot):
    return jnp.concatenate([gf[:, slot:slot + 1], gf[:, slot + 1:slot + 2],
                            gb[:, slot + 2:slot + 3], gb[:, slot + 3:slot + 4]], axis=0)


def _diag_blocks(x, rows0, width):
    return jnp.concatenate([x[rows0 + c * CHUNK:rows0 + (c + 1) * CHUNK, c * width:(c + 1) * width]
                            for c in range(4)], axis=0)


def _block_diag_place(v, r_col):
    chain = r_col // CHUNK
    return jnp.concatenate([jnp.where(chain == c, v, 0.0) for c in range(4)], axis=1)


def _chain_scale(col, width):
    blocks = []
    for c in range(4):
        t = jnp.broadcast_to(col[c * CHUNK:(c + 1) * CHUNK], (CHUNK, width))
        blocks.append(jnp.concatenate([t, t], axis=0))
    return jnp.concatenate(blocks, axis=1)


def _state_products(x, y, state, width):
    xs, ys = [], []
    for p in range(2):
        rows = slice(2 * p * CHUNK, (2 * p + 2) * CHUNK)
        out = _mm(jnp.concatenate([x[rows], y[rows]], axis=0), state[:, 2 * p * width:(2 * p + 2) * width])
        for e in range(2):
            cols = slice(e * width, (e + 1) * width)
            xs.append(out[e * CHUNK:(e + 1) * CHUNK, cols])
            ys.append(out[(2 + e) * CHUNK:(3 + e) * CHUNK, cols])
    return jnp.concatenate(xs, axis=0), jnp.concatenate(ys, axis=0)


def _scan_tile(b, k, reverse):
    lat = b * LAT_TILES + ((LAT_TILES - k) if reverse else (k - 1))
    return jnp.where(k == 0, BATCH * LAT_TILES + b, lat)


def _gdn_conv_kernel(xp_ref, xc_ref, xn_ref, cw_ref, o_ref, ext_ref):
    t = pl.program_id(0)
    ct = pl.program_id(1)
    _fill_halo(ext_ref, xp_ref, xc_ref, xn_ref, t, SEQ_TILE)
    qk_tiles = GDN_QK // GDN_CONV_TILE

    def conv_silu(h):
        cols = slice(h * GDN_DK, (h + 1) * GDN_DK)
        y = _conv4(ext_ref, cw_ref, SEQ_TILE, cols)
        return cols, y * jax.nn.sigmoid(y)

    @pl.when(ct >= 2 * qk_tiles)
    def _():
        for h in range(GDN_CONV_TILE // GDN_DK):
            cols, y = conv_silu(h)
            o_ref[:, cols] = y

    @pl.when(ct < 2 * qk_tiles)
    def _():
        scale = jnp.where(ct < qk_tiles, GDN_DK ** -0.5, 1.0)
        for h in range(GDN_CONV_TILE // GDN_DK):
            cols, y = conv_silu(h)
            nrm = y * lax.rsqrt(jnp.sum(y * y, axis=-1, keepdims=True) + EPS)
            o_ref[:, cols] = nrm * scale


GDN_CONV_TILE = 2048


def gdn_conv(p, conv_w):
    n = 2 * GDN_QK + GDN_VW
    tn = GDN_CONV_TILE
    return pl.pallas_call(
        _gdn_conv_kernel,
        grid=(N_SEQ_TILES, n // tn),
        in_specs=_halo_specs(tn, lambda t, c: c) + [pl.BlockSpec((CONV_W, tn), lambda t, c: (0, c))],
        out_specs=pl.BlockSpec((SEQ_TILE, tn), lambda t, c: (t, c)),
        out_shape=jax.ShapeDtypeStruct((T_ROWS, n), F32),
        scratch_shapes=[pltpu.VMEM((SEQ_TILE + 2 * SUBLANES, tn), F32)],
        compiler_params=_params("arbitrary", "arbitrary"),
        name="gdn_conv",
    )(p, p, p, conv_w)


def _row4(rf_ref, rb_ref, hp, rf, rb):
    return jnp.concatenate([rf_ref[hp, 0:1, rf:rf + CHUNK], rf_ref[hp, 1:2, rf:rf + CHUNK],
                            rb_ref[hp, 2:3, rb:rb + CHUNK], rb_ref[hp, 3:4, rb:rb + CHUNK]], axis=1)


GDN_STEP_PAIRS = 4
GDN_GROUP_CHUNKS = 1
GDN_STAGE_LAG = 2


def _gdn_core_kernel(qf_ref, kf_ref, vf_ref, gf_ref, grf_ref, qb_ref, kb_ref, vb_ref, gb_ref, grb_ref,
                     of_ref, ob_ref, s_ref):
    @pl.when(pl.program_id(2) == 0)
    def _():
        s_ref[...] = jnp.zeros_like(s_ref)

    r, s, incl, strict = _stack_masks()
    eye = r == s
    blk8 = (r // 8) == (s // 8)
    blk16 = (r // 16) == (s // 16)
    blk32 = (r // 32) == (s // 32)
    r_col = lax.broadcasted_iota(jnp.int32, (STACK, 1), 0)

    pairs = range(GDN_STEP_PAIRS)
    steps = range(CHUNKS_PER_TILE)
    rows_f = [n * CHUNK for n in steps]
    rows_b = [(CHUNKS_PER_TILE - 1 - n) * CHUNK for n in steps]
    problems = [(hp, n) for n in steps for hp in pairs]
    each = lambda fn, *lists: [fn(*xs) for xs in zip(*lists)]

    def load(hp, n):
        rf, rb = rows_f[n], rows_b[n]
        dk = slice(hp * GDN_DK, (hp + 1) * GDN_DK)
        v_of = lambda e: slice((2 * hp + e) * GDN_DV, (2 * hp + e + 1) * GDN_DV)
        kf = kf_ref[pl.ds(rf, CHUNK), dk]
        kb = kb_ref[pl.ds(rb, CHUNK), dk]
        qf = qf_ref[pl.ds(rf, CHUNK), dk]
        qb = qb_ref[pl.ds(rb, CHUNK), dk]
        k4 = jnp.concatenate([kf, kf, kb, kb], axis=0)
        q4 = jnp.concatenate([qf, qf, qb, qb], axis=0)
        v4 = jnp.concatenate([vf_ref[pl.ds(rf, CHUNK), v_of(0)], vf_ref[pl.ds(rf, CHUNK), v_of(1)],
                              vb_ref[pl.ds(rb, CHUNK), v_of(0)], vb_ref[pl.ds(rb, CHUNK), v_of(1)]], axis=0)
        gf = gf_ref[hp, pl.ds(rf, CHUNK), :]
        gb = gb_ref[hp, pl.ds(rb, CHUNK), :]
        return (k4, q4, v4, _col4(gf, gb, 0), _col4(gf, gb, 4), _col4(gf, gb, 12),
                _row4(grf_ref, grb_ref, hp, rf, rb))

    def gram_of(k, q):
        f, b = slice(0, CHUNK), slice(2 * CHUNK, 3 * CHUNK)
        g = _mm_nt(jnp.concatenate([k[f], k[b], q[f], q[b]], axis=0), k)
        return jnp.concatenate([g[i * CHUNK:(i + 1) * CHUNK] for i in (0, 0, 1, 1, 2, 2, 3, 3)], axis=0)

    def merge_fn(blk, level_mask):
        half = blk // 2
        starts = range(0, STACK, blk)
        active = [s0 + (half if s0 < 2 * CHUNK else 0) for s0 in starts]

        def left(a, x):
            rows = jnp.concatenate([a[s0:s0 + half] for s0 in active], axis=0)
            return _mm(rows, jnp.where(level_mask, x, 0.0))

        def right(a, y):
            z = _mm(y, a)
            out = []
            for i, s0 in enumerate(starts):
                upd = a[active[i]:active[i] + half] - z[i * half:(i + 1) * half]
                keep = a[s0:s0 + half] if active[i] != s0 else a[s0 + half:s0 + blk]
                out += [keep, upd] if active[i] != s0 else [upd, keep]
            return jnp.concatenate(out, axis=0)

        return left, right

    merge16 = merge_fn(16, jnp.logical_and(blk16, jnp.logical_not(blk8)))
    merge32 = merge_fn(32, jnp.logical_and(blk32, jnp.logical_not(blk16)))
    merge64 = merge_fn(CHUNK, jnp.logical_not(blk32))

    def phase_a(chunks):
        idx = [(hp, n) for n in chunks for hp in pairs]
        v = {}

        def s_load():
            v["k4"], v["q4"], v["v4"], v["beta"], v["gc"], v["gl"], v["gc_row"] = zip(*[load(hp, n) for hp, n in idx])
            v["eg"] = each(jnp.exp, v["gc"])
            v["gram"] = each(gram_of, v["k4"], v["q4"])

        def s_nmat():
            decay = each(lambda g, g_row: jnp.exp(jnp.where(incl, g - g_row, NEG_BIG)), v["gc"], v["gc_row"])
            v["nmat"] = each(lambda gm, d, b: jnp.where(strict, gm[:STACK] * d * b, 0.0),
                             v["gram"], decay, v["beta"])
            v["a_qk"] = each(lambda gm, d: gm[STACK:] * d, v["gram"], decay)
            v["n0"] = each(lambda x: jnp.where(blk8, x, 0.0), v["nmat"])
            v["t"] = each(lambda x: jnp.where(eye, 1.0, 0.0) - x, v["n0"])

        def s_p2():
            v["p2"] = each(_mm, v["n0"], v["n0"])

        def s_p4():
            v["p4"] = each(_mm, v["p2"], v["p2"])
            v["t"] = each(lambda a, b: a + _mm(a, b), v["t"], v["p2"])

        def s_t8():
            v["t"] = each(lambda a, b: a + _mm(a, b), v["t"], v["p4"])

        def s_left(merge):
            return lambda: v.__setitem__("y", each(merge[0], v["t"], v["nmat"]))

        def s_right(merge):
            return lambda: v.__setitem__("t", each(merge[1], v["t"], v["y"]))

        def s_uw():
            v["uw"] = each(lambda a, x, k, b, e: _mm(a, jnp.concatenate([x * b, k * (b * e)], axis=1)),
                           v["t"], v["v4"], v["k4"], v["beta"], v["eg"])
            v["qe"] = each(lambda q, e: q * e, v["q4"], v["eg"])
            v["k_t"] = each(lambda k, g, gt: (k * jnp.exp(gt - g)).T, v["k4"], v["gc"], v["gl"])
            v["s_scale"] = each(lambda gt: _chain_scale(jnp.exp(gt), GDN_DV), v["gl"])

        stages = [s_load, s_nmat, s_p2, s_p4, s_t8, s_left(merge16), s_right(merge16), s_left(merge32),
                  s_right(merge32), s_left(merge64), s_right(merge64), s_uw]
        return v, stages, {n: i * GDN_STEP_PAIRS for i, n in enumerate(chunks)}

    s_all = [s_ref[hp] for hp in pairs]

    def phase_b(group, n):
        v, _, base = group
        tmp = {}

        def s_state():
            for hp in pairs:
                i = base[n] + hp
                ws, qs = _state_products(v["uw"][i][:, GDN_DV:], v["qe"][i], s_all[hp], GDN_DV)
                tmp[hp] = (v["uw"][i][:, :GDN_DV] - ws, qs)

        def s_update():
            for hp in pairs:
                i = base[n] + hp
                v_new, qs = tmp[hp]
                o4 = qs + _mm(v["a_qk"][i], v_new)
                s_all[hp] = s_all[hp] * v["s_scale"][i] + _mm(v["k_t"][i], _block_diag_place(v_new, r_col))
                rf, rb = rows_f[n], rows_b[n]
                for c, (o_ref, r0) in enumerate(((of_ref, rf), (of_ref, rf), (ob_ref, rb), (ob_ref, rb))):
                    col = (2 * hp + c % 2) * GDN_DV
                    o_ref[pl.ds(r0, CHUNK), col:col + GDN_DV] = o4[c * CHUNK:(c + 1) * CHUNK]

        return [s_state, s_update]

    slots = {}
    for g in range(0, CHUNKS_PER_TILE, GDN_GROUP_CHUNKS):
        chunks = list(steps)[g:g + GDN_GROUP_CHUNKS]
        group = phase_a(chunks)
        sequence = group[1] + [st for n in chunks for st in phase_b(group, n)]
        for k, st in enumerate(sequence):
            slots.setdefault((g // GDN_GROUP_CHUNKS) * GDN_STAGE_LAG + k, []).append(st)
    for slot in sorted(slots):
        for st in slots[slot]:
            st()
    for hp in pairs:
        s_ref[hp] = s_all[hp]


def gdn_core(qkv, gcols, grows):
    hp = GDN_STEP_PAIRS
    kcol = GDN_QK // (hp * GDN_DK)
    vcol = 2 * GDN_QK // (hp * 2 * GDN_DV)

    def specs(reverse):
        tile = lambda b, j, k: _scan_tile(b, k, reverse)
        return [pl.BlockSpec((SEQ_TILE, hp * GDN_DK), lambda b, j, k: (tile(b, j, k), j)),
                pl.BlockSpec((SEQ_TILE, hp * GDN_DK), lambda b, j, k: (tile(b, j, k), kcol + j)),
                pl.BlockSpec((SEQ_TILE, hp * 2 * GDN_DV), lambda b, j, k: (tile(b, j, k), vcol + j)),
                pl.BlockSpec((hp, SEQ_TILE, PAIR_SLOTS), lambda b, j, k: (j, tile(b, j, k), 0)),
                pl.BlockSpec((hp, 4, SEQ_TILE), lambda b, j, k: (j, 0, tile(b, j, k)))]

    out_spec = lambda reverse: pl.BlockSpec((SEQ_TILE, hp * 2 * GDN_DV),
                                            lambda b, j, k: (_scan_tile(b, k, reverse), j))
    return pl.pallas_call(
        _gdn_core_kernel,
        grid=(BATCH, GDN_QK_HEADS // hp, LAT_TILES + 1),
        in_specs=specs(False) + specs(True),
        out_specs=[out_spec(False), out_spec(True)],
        out_shape=[jax.ShapeDtypeStruct((T_ROWS, GDN_VW), F32)] * 2,
        scratch_shapes=[pltpu.VMEM((hp, GDN_DK, 4 * GDN_DV), F32)],
        compiler_params=_params("arbitrary", "arbitrary", "arbitrary"),
        name="gdn_core",
    )(qkv, qkv, qkv, gcols, grows, qkv, qkv, qkv, gcols, grows)


def _head_norm_gate_kernel(of_ref, ob_ref, z_ref, nw_ref, y_ref, *, head_dim, gate):
    for h in range(of_ref.shape[1] // head_dim):
        sl = slice(h * head_dim, (h + 1) * head_dim)
        o = of_ref[:, sl] + ob_ref[:, sl]
        y = o * lax.rsqrt(jnp.mean(o * o, axis=-1, keepdims=True) + EPS) * nw_ref[:, sl]
        z = z_ref[:, sl]
        g = z * jax.nn.sigmoid(z) if gate == "silu" else jax.nn.sigmoid(z)
        y_ref[:, sl] = (y * g).astype(BF16)


def head_norm_gate(o_f, o_b, p, z_col0, norm_row, head_dim, gate):
    width = o_f.shape[1]
    tm, tn = 512, 1024
    assert z_col0 % tn == 0 and width % tn == 0
    z_col_block = z_col0 // tn
    spec = pl.BlockSpec((tm, tn), lambda t, c: (t, c))
    return pl.pallas_call(
        functools.partial(_head_norm_gate_kernel, head_dim=head_dim, gate=gate),
        grid=(T_ROWS // tm, width // tn),
        in_specs=[spec, spec, pl.BlockSpec((tm, tn), lambda t, c: (t, z_col_block + c)),
                  pl.BlockSpec((1, tn), lambda t, c: (0, c))],
        out_specs=spec,
        out_shape=jax.ShapeDtypeStruct((T_ROWS, width), BF16),
        compiler_params=_params("arbitrary", "arbitrary"),
        name="head_norm_" + gate,
    )(o_f, o_b, p, norm_row)


def gdn_mixer(p, gates, conv_w, a_log, dt_bias, norm_w):
    zeros4 = jnp.zeros((GDN_QK_HEADS, 4), F32)
    per_pair = lambda t: t.reshape(2, GDN_QK_HEADS, 2).transpose(1, 0, 2).reshape(GDN_QK_HEADS, 4)
    row = lambda t: jnp.concatenate([zeros4, per_pair(t)], axis=1).reshape(1, LANES)
    gcols, grows = gate_columns(gates, row(a_log), row(dt_bias), "gdn", GDN_QK_HEADS)
    qkv = gdn_conv(p, conv_w)
    o_f, o_b = gdn_core(qkv, gcols, grows)
    norm_row = jnp.tile(norm_w, GDN_V_HEADS).reshape(1, GDN_VW)
    return head_norm_gate(o_f, o_b, p, 2 * GDN_QK + GDN_VW, norm_row, GDN_DV, "silu")


GDN_MAIN_COLS = 2 * GDN_QK + 2 * GDN_VW


def gdn_gate_weights(w_in):
    gates = w_in[:, GDN_MAIN_COLS:].reshape(D_MODEL, 2, 2, GDN_QK_HEADS, 2)
    return gates.transpose(0, 3, 1, 2, 4).reshape(D_MODEL, 4 * GDN_V_HEADS).astype(BF16)


ML_PAIRS = ML_HEADS // 2


def _mlstm_core_kernel(qf_ref, kf_ref, vf_ref, gf_ref, grf_ref, qb_ref, kb_ref, vb_ref, gb_ref, grb_ref,
                       of_ref, ob_ref, c_ref, n_ref, m_ref):
    @pl.when(pl.program_id(2) == 0)
    def _():
        c_ref[...] = jnp.zeros_like(c_ref)
        n_ref[...] = jnp.zeros_like(n_ref)
        m_ref[...] = jnp.zeros_like(m_ref)

    r, s, incl, _ = _stack_masks()
    r_col = lax.broadcasted_iota(jnp.int32, (STACK, 1), 0)
    last_rows = (CHUNK - 1, 2 * CHUNK - 1, 2 * CHUNK, 3 * CHUNK)

    steps = range(CHUNKS_PER_TILE)
    rows_f = [n * CHUNK for n in steps]
    rows_b = [(CHUNKS_PER_TILE - 1 - n) * CHUNK for n in steps]
    each = lambda fn, *lists: [fn(*xs) for xs in zip(*lists)]

    def load(rf, rb):
        def stack(f_ref, b_ref, w):
            return jnp.concatenate([f_ref[pl.ds(rf, CHUNK), 0:w], f_ref[pl.ds(rf, CHUNK), w:2 * w],
                                    b_ref[pl.ds(rb, CHUNK), 0:w], b_ref[pl.ds(rb, CHUNK), w:2 * w]], axis=0)

        gf = gf_ref[0, pl.ds(rf, CHUNK), :]
        gb = gb_ref[0, pl.ds(rb, CHUNK), :]
        return (stack(qf_ref, qb_ref, ML_DQK) * ML_DQK ** -0.5, stack(kf_ref, kb_ref, ML_DQK),
                stack(vf_ref, vb_ref, ML_DV), _col4(gf, gb, 0), _col4(gf, gb, 4), _col4(gf, gb, 12),
                _row4(grf_ref, grb_ref, 0, rf, rb))

    q4, k4, v4, ig, bc, bl, ib_row = zip(*each(load, rows_f, rows_b))
    d_log = each(lambda b, row: jnp.where(incl, b + row, NEG_BIG), bc, ib_row)
    row_max = each(lambda d: jnp.max(d, axis=1, keepdims=True), d_log)
    w_max = each(lambda rm: jnp.concatenate([jnp.broadcast_to(rm[i:i + 1], (CHUNK, 1)) for i in last_rows], axis=0),
                 row_max)
    qk = each(_mm_nt, q4, k4)

    m_in = [m_ref[...]]
    for n in steps:
        m_in.append(jnp.maximum(bl[n] + m_in[n], w_max[n]))
    m_out = m_in[1:]
    m_ref[...] = m_in[-1]

    inter = each(lambda b, m: b + m, bc, m_in)
    m_t = each(jnp.maximum, inter, row_max)
    smat = each(lambda x, d, mt: x * jnp.exp(d - mt), qk, d_log, m_t)
    w_inter = each(lambda i, mt: jnp.exp(i - mt), inter, m_t)
    intra = each(_mm, smat, v4)
    s_sum = each(lambda x: jnp.sum(x, axis=1, keepdims=True), smat)
    cd = each(lambda b, m0, m1: jnp.exp(b + m0 - m1), bl, m_in, m_out)
    kw = each(lambda k, b, c, i, m1: k * jnp.exp(b - c + i - m1), k4, bl, bc, ig, m_out)
    d_c = each(lambda x, v: _mm(x.T, _block_diag_place(v, r_col)), kw, v4)
    c_scale = each(lambda x: _chain_scale(x, ML_DV), cd)

    c_all = c_ref[...]
    n_all = [n_ref[c:c + 1, :] for c in range(4)]
    for n in steps:
        qc = []
        for p in range(2):
            out = _mm(q4[n][2 * p * CHUNK:(2 * p + 2) * CHUNK], c_all[:, 2 * p * ML_DV:(2 * p + 2) * ML_DV])
            qc += [out[e * CHUNK:(e + 1) * CHUNK, e * ML_DV:(e + 1) * ML_DV] for e in range(2)]
        qc = jnp.concatenate(qc, axis=0)
        num = w_inter[n] * qc + intra[n]
        n_rows = jnp.concatenate([jnp.broadcast_to(x, (CHUNK, ML_DQK)) for x in n_all], axis=0)
        den = w_inter[n] * jnp.sum(q4[n] * n_rows, axis=1, keepdims=True) + s_sum[n]
        h4 = num / jnp.maximum(jnp.abs(den), jnp.exp(-m_t[n]))
        c_all = c_all * c_scale[n] + d_c[n]
        n_all = [cd[n][c * CHUNK:c * CHUNK + 1] * n_all[c]
                 + jnp.sum(kw[n][c * CHUNK:(c + 1) * CHUNK], axis=0, keepdims=True) for c in range(4)]
        rf, rb = rows_f[n], rows_b[n]
        of_ref[pl.ds(rf, CHUNK), 0:ML_DV] = h4[0:CHUNK]
        of_ref[pl.ds(rf, CHUNK), ML_DV:2 * ML_DV] = h4[CHUNK:2 * CHUNK]
        ob_ref[pl.ds(rb, CHUNK), 0:ML_DV] = h4[2 * CHUNK:3 * CHUNK]
        ob_ref[pl.ds(rb, CHUNK), ML_DV:2 * ML_DV] = h4[3 * CHUNK:4 * CHUNK]
    c_ref[...] = c_all
    for c in range(4):
        n_ref[c:c + 1, :] = n_all[c]


def mlstm_core(p, gcols, grows):
    kcol = ML_QK // (2 * ML_DQK)
    vcol = 2 * ML_QK // (2 * ML_DV)

    def specs(reverse):
        tile = lambda b, j, k: _scan_tile(b, k, reverse)
        return [pl.BlockSpec((SEQ_TILE, 2 * ML_DQK), lambda b, j, k: (tile(b, j, k), j)),
                pl.BlockSpec((SEQ_TILE, 2 * ML_DQK), lambda b, j, k: (tile(b, j, k), kcol + j)),
                pl.BlockSpec((SEQ_TILE, 2 * ML_DV), lambda b, j, k: (tile(b, j, k), vcol + j)),
                pl.BlockSpec((1, SEQ_TILE, PAIR_SLOTS), lambda b, j, k: (j, tile(b, j, k), 0)),
                pl.BlockSpec((1, 4, SEQ_TILE), lambda b, j, k: (j, 0, tile(b, j, k)))]

    out_spec = lambda reverse: pl.BlockSpec((SEQ_TILE, 2 * ML_DV),
                                            lambda b, j, k: (_scan_tile(b, k, reverse), j))
    return pl.pallas_call(
        _mlstm_core_kernel,
        grid=(BATCH, ML_PAIRS, LAT_TILES + 1),
        in_specs=specs(False) + specs(True),
        out_specs=[out_spec(False), out_spec(True)],
        out_shape=[jax.ShapeDtypeStruct((T_ROWS, ML_V), F32)] * 2,
        scratch_shapes=[pltpu.VMEM((ML_DQK, 4 * ML_DV), F32), pltpu.VMEM((SUBLANES, ML_DQK), F32),
                        pltpu.VMEM((STACK, 1), F32)],
        compiler_params=_params("arbitrary", "arbitrary", "arbitrary"),
        name="mlstm_core",
    )(p, p, p, gcols, grows, p, p, p, gcols, grows)


def mlstm_mixer(p, gates, gate_b, norm_w):
    bias = gate_b.reshape(2, 2, ML_PAIRS, 2).transpose(2, 1, 0, 3).reshape(1, 4 * ML_HEADS)
    bias = jnp.pad(bias, ((0, 0), (0, LANES - 4 * ML_HEADS)))
    gcols, grows = gate_columns(gates, bias, bias, "mlstm", ML_PAIRS)
    h_f, h_b = mlstm_core(p, gcols, grows)
    return head_norm_gate(h_f, h_b, p, 2 * ML_QK + ML_V, norm_w.reshape(1, ML_V), ML_DV, "sigmoid")


def mlstm_in_weights(w_in):
    base = 2 * ML_QK + 2 * ML_V
    gates = w_in[:, base:].reshape(D_MODEL, 2, 2, ML_PAIRS, 2)
    gates = gates.transpose(0, 3, 2, 1, 4).reshape(D_MODEL, 4 * ML_HEADS)
    return w_in[:, :base].astype(BF16), _pad_cols(gates, LANES).astype(BF16)


def _pad_cols(w, mult):
    n = w.shape[1]
    return jnp.pad(w, ((0, 0), (0, (-n) % mult)))


def kernel(x, c, ctx, c_ctx, w_mod, b_mod, norm_mix, norm_ff, norm_out, gdn_w_in, gdn_conv, gdn_a_log, gdn_dt_bias, gdn_norm, gdn_w_out, ml_w_in, ml_gate_b, ml_norm, ml_w_out, lru_w_in, lru_conv, lru_conv_b, lru_w_gate, lru_b_gate, lru_lambda, lru_w_out, ff_w_up, ff_w_down):
    assert DEPTH % 2 == 0
    r_lat = x.reshape(N_LAT, D_MODEL)
    r_ctx = ctx.reshape(N_CTX, D_MODEL)
    c_all = jnp.concatenate([c, c_ctx[None], jnp.zeros((MOD_ROWS - BATCH - 1, D_MODEL), F32)], axis=0)
    mod = modulation_table(c_all, w_mod, b_mod)
    gdn_w_in_bf16 = gdn_w_in.astype(BF16)
    for i in range(DEPTH):
        kind, j = i % N_MIXERS, i // N_MIXERS
        if kind == 0:
            p, gates = in_proj(r_lat, r_ctx, norm_mix[i], mod[i], gdn_w_in_bf16, gdn_gate_weights(gdn_w_in[j]),
                               layer=j, n=GDN_MAIN_COLS)
            y = gdn_mixer(p, gates, gdn_conv[j], gdn_a_log[j], gdn_dt_bias[j], gdn_norm[j])
            w_out = gdn_w_out[j]
        elif kind == 1:
            p, gates = in_proj(r_lat, r_ctx, norm_mix[i], mod[i], *mlstm_in_weights(ml_w_in[j]))
            y = mlstm_mixer(p, gates, ml_gate_b[j], ml_norm[j])
            w_out = ml_w_out[j]
        else:
            no_gates = jnp.zeros((D_MODEL, LANES), BF16)
            p, _ = in_proj(r_lat, r_ctx, norm_mix[i], mod[i], lru_w_in[j].astype(BF16), no_gates)
            y = lru_mixer(p, lru_conv[j], lru_conv_b[j], lru_w_gate[j], lru_b_gate[j], lru_lambda[j])
            w_out = lru_w_out[j]
        w_out, w_up, w_down = w_out.astype(BF16), ff_w_up[i].astype(BF16), ff_w_down[i].astype(BF16)
        r_lat = out_proj(y, w_out, r_lat, mod[i], is_ctx=False)
        r_lat = mlp(r_lat, norm_ff[i], mod[i], w_up, w_down, is_ctx=False)
        if i < DEPTH - 1:
            r_ctx = out_proj(y, w_out, r_ctx, mod[i], is_ctx=True)
            r_ctx = mlp(r_ctx, norm_ff[i], mod[i], w_up, w_down, is_ctx=True)
    return final_norm(r_lat, norm_out).reshape(BATCH, SEQ, D_MODEL)
```
